```python
import jax, jax.numpy as jnp
from jax import lax
import numpy as np

D_MODEL = 1024
BATCH = 2
SEQ = 8192
DEPTH = 2

HEAD_DIM = 64
FOX_HEADS = 8
NSA_HEADS = 8
NSA_KV_HEADS = 2
NSA_GROUP = NSA_HEADS // NSA_KV_HEADS
CMP_BLOCK = 32
CMP_STRIDE = 16
CMP_HIDDEN = 4 * HEAD_DIM
SEL_BLOCK = 64
SEL_TOPK = 16
WINDOW = 512
Q_BLOCK = 128
ROPE_THETA = 500000.0
ROPE_DIM = HEAD_DIM // 4
D_FF = 4 * D_MODEL
RMS_EPS = 1e-6
FORGET_BIAS_INIT = 3.0
FORCE_SCORE = 1e6
NEG_BIG = -1e30

FOX_W = FOX_HEADS * HEAD_DIM
NSA_W = NSA_HEADS * HEAD_DIM
KV_W = NSA_KV_HEADS * HEAD_DIM
SPLITS = (FOX_W, FOX_W, FOX_W, FOX_HEADS, NSA_W, 6 * KV_W, 3 * NSA_HEADS, D_MODEL, D_MODEL)
D_IN = sum(SPLITS)

kernel_name = 'hybrid_fox_nsa_block'


def rmsnorm(x, g):
    xf = x.astype(jnp.float32)
    y = xf * lax.rsqrt(jnp.mean(xf * xf, axis=-1, keepdims=True) + RMS_EPS)
    return (y * g.astype(jnp.float32)).astype(x.dtype)


def partial_rope(t, pos):
    half = ROPE_DIM // 2
    inv = ROPE_THETA ** (-jnp.arange(half, dtype=jnp.float32) / half)
    ang = pos.astype(jnp.float32)[:, None] * inv[None, :]
    cos, sin = jnp.cos(ang), jnp.sin(ang)
    x1 = t[..., :half].astype(jnp.float32)
    x2 = t[..., half:ROPE_DIM].astype(jnp.float32)
    rot = jnp.concatenate([x1 * cos - x2 * sin, x2 * cos + x1 * sin], axis=-1)
    return jnp.concatenate([rot.astype(t.dtype), t[..., ROPE_DIM:]], axis=-1)


def to_heads(t, n):
    B, T, _ = t.shape
    return t.reshape(B, T, n, HEAD_DIM).transpose(0, 2, 1, 3)


def from_heads(o):
    B, H, T, dh = o.shape
    return o.transpose(0, 2, 1, 3).reshape(B, T, H * dh)


def n_cmp_blocks(T):
    return (T - CMP_BLOCK) // CMP_STRIDE + 1


def cmp_end_positions(T):
    return jnp.arange(n_cmp_blocks(T)) * CMP_STRIDE + CMP_BLOCK - 1


def cmp_to_sel_matrix(T):
    nc, n_sel = n_cmp_blocks(T), T // SEL_BLOCK
    cs = np.arange(nc) * CMP_STRIDE
    ce = cs + CMP_BLOCK
    ss = np.arange(n_sel) * SEL_BLOCK
    se = ss + SEL_BLOCK
    ov = np.clip(np.minimum(ce[:, None], se[None, :]) - np.maximum(cs[:, None], ss[None, :]), 0, None)
    return (ov / CMP_BLOCK).astype(np.float32)


def fox_attention(q, k, v, log_f):
    B, H, T, dh = q.shape
    c = jnp.cumsum(log_f, axis=-1)
    nb = T // Q_BLOCK
    qb = q.reshape(B, H, nb, Q_BLOCK, dh).transpose(2, 0, 1, 3, 4)
    cb = c.reshape(B, H, nb, Q_BLOCK).transpose(2, 0, 1, 3)
    kpos = jnp.arange(T)
    scale = HEAD_DIM ** -0.5

    def block(args):
        i, qi, ci = args
        qpos = i * Q_BLOCK + jnp.arange(Q_BLOCK)
        s = jnp.einsum('bhqd,bhkd->bhqk', qi, k, preferred_element_type=jnp.float32) * scale
        s = s + ci[..., :, None] - c[..., None, :]
        s = jnp.where(kpos[None, :] <= qpos[:, None], s, -jnp.inf)
        p = jax.nn.softmax(s, axis=-1)
        return jnp.einsum('bhqk,bhkd->bhqd', p.astype(v.dtype), v)

    out = lax.map(block, (jnp.arange(nb), qb, cb))
    return out.transpose(1, 2, 0, 3, 4).reshape(B, H, T, dh)


def compress_blocks(t, pos_emb, w1, b1, w2, b2):
    B, G, T, dh = t.shape
    nc = n_cmp_blocks(T)
    idx = np.arange(nc)[:, None] * CMP_STRIDE + np.arange(CMP_BLOCK)[None, :]
    blocks = t[:, :, idx, :] + pos_emb
    flat = blocks.reshape(B, G, nc, CMP_BLOCK * dh)
    return jax.nn.gelu(flat @ w1 + b1) @ w2 + b2


def nsa_attention(q, kc, vc, ks, vs, kw, vw):
    B, G, R, T, dh = q.shape
    n_sel = T // SEL_BLOCK
    top = min(SEL_TOPK, n_sel)
    nb = T // Q_BLOCK
    scale = HEAD_DIM ** -0.5
    cmp_end = cmp_end_positions(T)
    sel_map = jnp.asarray(cmp_to_sel_matrix(T))
    ks_blocks = ks.reshape(B, G, n_sel, SEL_BLOCK, dh)
    vs_blocks = vs.reshape(B, G, n_sel, SEL_BLOCK, dh)
    pad = ((0, 0), (0, 0), (WINDOW, 0), (0, 0))
    kw_pad = jnp.pad(kw, pad)
    vw_pad = jnp.pad(vw, pad)
    gather = jax.vmap(jax.vmap(lambda blocks, ix: blocks[ix]))
    qb = q.reshape(B, G, R, nb, Q_BLOCK, dh).transpose(3, 0, 1, 2, 4, 5)
    sel_ids = jnp.arange(n_sel)
    blk_start = sel_ids * SEL_BLOCK

    def block(args):
        i, qi = args
        q0 = i * Q_BLOCK
        qpos = q0 + jnp.arange(Q_BLOCK)
        s = jnp.einsum('bgrqd,bgnd->bgrqn', qi, kc, preferred_element_type=jnp.float32) * scale
        valid_c = cmp_end[None, :] <= qpos[:, None]
        p_c = jax.nn.softmax(jnp.where(valid_c, s, NEG_BIG), axis=-1) * valid_c
        o_c = jnp.einsum('bgrqn,bgnd->bgrqd', p_c.astype(vc.dtype), vc)
        imp = jnp.einsum('bgrqn,ns->bgqs', p_c, sel_map)
        future = blk_start[None, :] > qpos[:, None]
        forced = (sel_ids[None, :] == (qpos // SEL_BLOCK)[:, None]) | (sel_ids[None, :] == 0)
        imp = jnp.where(future, -1.0, jnp.where(forced, FORCE_SCORE, imp))
        _, idx = lax.top_k(imp, top)
        kg = gather(ks_blocks, idx).reshape(B, G, Q_BLOCK, top * SEL_BLOCK, dh)
        vg = gather(vs_blocks, idx).reshape(B, G, Q_BLOCK, top * SEL_BLOCK, dh)
        kpos = (idx[..., None] * SEL_BLOCK + jnp.arange(SEL_BLOCK)).reshape(B, G, Q_BLOCK, top * SEL_BLOCK)
        valid_s = kpos <= qpos[None, None, :, None]
        s = jnp.einsum('bgrqd,bgqkd->bgrqk', qi, kg, preferred_element_type=jnp.float32) * scale
        p_s = jax.nn.softmax(jnp.where(valid_s[:, :, None], s, -jnp.inf), axis=-1)
        o_s = jnp.einsum('bgrqk,bgqkd->bgrqd', p_s.astype(vg.dtype), vg)
        kwin = lax.dynamic_slice_in_dim(kw_pad, q0, WINDOW + Q_BLOCK, axis=2)
        vwin = lax.dynamic_slice_in_dim(vw_pad, q0, WINDOW + Q_BLOCK, axis=2)
        kpos_w = q0 - WINDOW + jnp.arange(WINDOW + Q_BLOCK)
        diff = qpos[:, None] - kpos_w[None, :]
        valid_w = (kpos_w[None, :] >= 0) & (diff >= 0) & (diff < WINDOW)
        s = jnp.einsum('bgrqd,bgkd->bgrqk', qi, kwin, preferred_element_type=jnp.float32) * scale
        p_w = jax.nn.softmax(jnp.where(valid_w, s, -jnp.inf), axis=-1)
        o_w = jnp.einsum('bgrqk,bgkd->bgrqd', p_w.astype(vwin.dtype), vwin)
        return o_c, o_s, o_w

    o_c, o_s, o_w = lax.map(block, (jnp.arange(nb), qb))
    unblock = lambda o: o.transpose(1, 2, 3, 0, 4, 5).reshape(B, G, R, T, dh)
    return unblock(o_c), unblock(o_s), unblock(o_w)


def hybrid_layer(x, norm_mix, w_in, b_forget, cmp_pos, cmp_w1, cmp_b1, cmp_w2, cmp_b2,
                 w_o_fox, w_o_nsa, w_out, norm_mlp, w_up, w_down):
    B, T, _ = x.shape
    h = rmsnorm(x, norm_mix)
    proj = h @ w_in
    q_a, k_a, v_a, f_a, q_b, kv_b, g_b, gate_a, gate_b = jnp.split(
        proj, np.cumsum(SPLITS)[:-1].tolist(), axis=-1)
    log_f = jax.nn.log_sigmoid((f_a + b_forget).astype(jnp.float32)).transpose(0, 2, 1)
    o_a = fox_attention(to_heads(q_a, FOX_HEADS), to_heads(k_a, FOX_HEADS), to_heads(v_a, FOX_HEADS), log_f)
    y_a = from_heads(o_a) @ w_o_fox
    pos = jnp.arange(T)
    q_n = partial_rope(to_heads(q_b, NSA_HEADS), pos).reshape(B, NSA_KV_HEADS, NSA_GROUP, T, HEAD_DIM)
    kc, vc, ks, vs, kw, vw = [to_heads(t, NSA_KV_HEADS) for t in jnp.split(kv_b, 6, axis=-1)]
    kc = compress_blocks(kc, cmp_pos[0], cmp_w1[0], cmp_b1[0], cmp_w2[0], cmp_b2[0])
    kc = partial_rope(kc, cmp_end_positions(T))
    vc = compress_blocks(vc, cmp_pos[1], cmp_w1[1], cmp_b1[1], cmp_w2[1], cmp_b2[1])
    ks = partial_rope(ks, pos)
    kw = partial_rope(kw, pos)
    o_c, o_s, o_w = nsa_attention(q_n, kc, vc, ks, vs, kw, vw)
    g = jax.nn.sigmoid(g_b).reshape(B, T, NSA_HEADS, 3).transpose(0, 2, 1, 3)
    g = g.reshape(B, NSA_KV_HEADS, NSA_GROUP, T, 3)
    o_n = g[..., 0:1] * o_c + g[..., 1:2] * o_s + g[..., 2:3] * o_w
    y_b = from_heads(o_n.reshape(B, NSA_HEADS, T, HEAD_DIM)) @ w_o_nsa
    mixed = jax.nn.sigmoid(gate_a) * y_a + jax.nn.sigmoid(gate_b) * y_b
    x = x + mixed @ w_out
    h = rmsnorm(x, norm_mlp)
    return x + jnp.square(jax.nn.relu(h @ w_up)) @ w_down


def setup_inputs(seed: int = 0) -> dict:
    key = jax.random.key(seed)
    ks = jax.random.split(key, 17)
    f32 = jnp.float32
    nrm = lambda k, shape, fan_in: jax.random.normal(k, shape, f32) * fan_in ** -0.5
    return {
        'x': jax.random.normal(ks[0], (BATCH, SEQ, D_MODEL), f32),
        'norm_mix': 1.0 + 0.02 * jax.random.normal(ks[1], (DEPTH, D_MODEL), f32),
        'w_in': nrm(ks[2], (DEPTH, D_MODEL, D_IN), D_MODEL),
        'b_forget': FORGET_BIAS_INIT + 0.1 * jax.random.normal(ks[3], (DEPTH, FOX_HEADS), f32),
        'cmp_pos': 0.02 * jax.random.normal(ks[4], (DEPTH, 2, CMP_BLOCK, HEAD_DIM), f32),
        'cmp_w1': nrm(ks[5], (DEPTH, 2, CMP_BLOCK * HEAD_DIM, CMP_HIDDEN), CMP_BLOCK * HEAD_DIM),
        'cmp_b1': 0.02 * jax.random.normal(ks[6], (DEPTH, 2, CMP_HIDDEN), f32),
        'cmp_w2': nrm(ks[7], (DEPTH, 2, CMP_HIDDEN, HEAD_DIM), CMP_HIDDEN),
        'cmp_b2': 0.02 * jax.random.normal(ks[8], (DEPTH, 2, HEAD_DIM), f32),
        'w_o_fox': nrm(ks[9], (DEPTH, FOX_W, D_MODEL), FOX_W),
        'w_o_nsa': nrm(ks[10], (DEPTH, NSA_W, D_MODEL), NSA_W),
        'w_out': nrm(ks[11], (DEPTH, D_MODEL, D_MODEL), D_MODEL),
        'norm_mlp': 1.0 + 0.02 * jax.random.normal(ks[12], (DEPTH, D_MODEL), f32),
        'w_up': nrm(ks[13], (DEPTH, D_MODEL, D_FF), D_MODEL),
        'w_down': 0.5 * nrm(ks[14], (DEPTH, D_FF, D_MODEL), D_FF),
        'norm_final': 1.0 + 0.02 * jax.random.normal(ks[15], (D_MODEL,), f32),
    }


def reference(x, norm_mix, w_in, b_forget, cmp_pos, cmp_w1, cmp_b1, cmp_w2, cmp_b2,
              w_o_fox, w_o_nsa, w_out, norm_mlp, w_up, w_down, norm_final):
    for l in range(DEPTH):
        x = hybrid_layer(x, norm_mix[l], w_in[l], b_forget[l], cmp_pos[l], cmp_w1[l], cmp_b1[l],
                         cmp_w2[l], cmp_b2[l], w_o_fox[l], w_o_nsa[l], w_out[l],
                         norm_mlp[l], w_up[l], w_down[l])
    return rmsnorm(x, norm_final)
```

```python
import functools

import numpy as np
import jax
import jax.numpy as jnp
from jax import lax
from jax.experimental import pallas as pl
from jax.experimental.pallas import tpu as pltpu

F32 = jnp.float32
BF16 = jnp.bfloat16

D_MODEL = 1024
HEAD_DIM = 64
FOX_HEADS = 8
NSA_HEADS = 8
NSA_KV_HEADS = 2
NSA_GROUP = NSA_HEADS // NSA_KV_HEADS
CMP_BLOCK = 32
CMP_STRIDE = 16
CMP_HIDDEN = 4 * HEAD_DIM
SEL_BLOCK = 64
SEL_TOPK = 16
WINDOW = 512
ROPE_THETA = 500000.0
ROPE_DIM = HEAD_DIM // 4
ROPE_HALF = ROPE_DIM // 2
D_FF = 4 * D_MODEL
RMS_EPS = 1e-6
FORCE_SCORE = 1e6
NEG_BIG = -1e30
SCALE = HEAD_DIM ** -0.5

FOX_W = FOX_HEADS * HEAD_DIM
NSA_W = NSA_HEADS * HEAD_DIM
KV_W = NSA_KV_HEADS * HEAD_DIM

LANES = 128
VMEM_LIMIT_BYTES = 56 * 1024 * 1024

FOX_PAD_W = FOX_HEADS * LANES
GROUP_W = NSA_GROUP * HEAD_DIM
OFF_QA = 0
OFF_KA = OFF_QA + FOX_PAD_W
OFF_VA = OFF_KA + FOX_PAD_W
OFF_F = OFF_VA + FOX_PAD_W
OFF_QB = OFF_F + LANES
OFF_KC = OFF_QB + NSA_W
OFF_VC = OFF_KC + KV_W
OFF_KS = OFF_VC + KV_W
OFF_VS = OFF_KS + NSA_KV_HEADS * GROUP_W
OFF_KW = OFF_VS + NSA_KV_HEADS * LANES
OFF_VW = OFF_KW + NSA_KV_HEADS * GROUP_W
OFF_GB = OFF_VW + NSA_KV_HEADS * LANES
OFF_GATE = OFF_GB + LANES
PROJ_W = OFF_GATE + 2 * D_MODEL

AUG_Q_C = HEAD_DIM
AUG_K_C = HEAD_DIM + 3

PROJ_TM = 256
FOX_TQ = 256
FOX_TK = 512
NSA_TQ = 256
SEL_TK = 512
MERGE_TM = 256
MLP_TM = 512
MLP_FF_CHUNK = 512


def _cparams(sem):
    return pltpu.CompilerParams(dimension_semantics=sem, vmem_limit_bytes=VMEM_LIMIT_BYTES)


def _dot(a, b):
    return jnp.dot(a, b, preferred_element_type=F32)


def _dot_nt(a, b):
    return lax.dot_general(a, b, (((1,), (1,)), ((), ())), preferred_element_type=F32)


def _split3(x):
    hi = x.astype(BF16)
    r = x - hi.astype(F32)
    mid = r.astype(BF16)
    lo = (r - mid.astype(F32)).astype(BF16)
    return hi, mid, lo


def _lane_tile(x, reps):
    return x if reps == 1 else jnp.concatenate([x] * reps, axis=1)


def _rope(y, rc, rs1, rs2):
    n = y.shape[1]
    return y * rc + pltpu.roll(y, n - ROPE_HALF, 1) * rs1 + pltpu.roll(y, ROPE_HALF, 1) * rs2


def _rmsnorm(x, g):
    return x * lax.rsqrt(jnp.mean(x * x, axis=-1, keepdims=True) + RMS_EPS) * g


def _sigmoid(x):
    return 1.0 / (1.0 + jnp.exp(-x))


def _proj_kernel(x_ref, g_ref, w_ref, bf_ref, pq_ref, pk_ref, rc_ref, rs1_ref, rs2_ref,
                 qa_ref, ka_ref, va_ref, qb_ref, ks_ref, kw_ref, vs_ref, vw_ref, kvc_ref, gb_ref, gate_ref,
                 h_scr, carry_scr, *, tm):
    @pl.when(pl.program_id(1) == 0)
    def _():
        carry_scr[...] = jnp.zeros_like(carry_scr)

    h_scr[...] = _rmsnorm(x_ref[0], g_ref[...]).astype(BF16)

    def seg(c0, n):
        return _dot(h_scr[...], w_ref[:, c0:c0 + n])

    f = seg(OFF_F, LANES) + bf_ref[...]
    lf = jnp.minimum(f, 0.0) - jnp.log1p(jnp.exp(-jnp.abs(f)))
    row = lax.broadcasted_iota(jnp.int32, (tm, tm), 0)
    col = lax.broadcasted_iota(jnp.int32, (tm, tm), 1)
    tri = jnp.where(col <= row, 1.0, 0.0).astype(BF16)
    hi, mid, lo = _split3(lf)
    c = _dot(tri, hi) + _dot(tri, mid) + _dot(tri, lo) + carry_scr[0:1, :]
    carry_scr[0:1, :] = c[tm - 1:tm, :]
    c3 = jnp.concatenate(_split3(c), axis=1)

    lane = lax.broadcasted_iota(jnp.int32, (tm, FOX_PAD_W), 1) & (LANES - 1)
    ones_q = jnp.where((lane >= AUG_K_C) & (lane < AUG_K_C + 3), 1.0, 0.0)
    ones_k = jnp.where((lane >= AUG_Q_C) & (lane < AUG_Q_C + 3), 1.0, 0.0)
    qa_ref[0] = (seg(OFF_QA, FOX_PAD_W) + _dot(c3, pq_ref[...]) + ones_q).astype(BF16)
    ka_ref[0] = (seg(OFF_KA, FOX_PAD_W) + _dot(c3, pk_ref[...]) + ones_k).astype(BF16)
    va_ref[0] = seg(OFF_VA, FOX_PAD_W).astype(BF16)

    reps = NSA_W // LANES
    rc = _lane_tile(rc_ref[...], reps)
    rs1 = _lane_tile(rs1_ref[...], reps)
    rs2 = _lane_tile(rs2_ref[...], reps)
    qb_ref[0] = _rope(seg(OFF_QB, NSA_W), rc, rs1, rs2).astype(BF16)
    ks_ref[0] = _rope(seg(OFF_KS, NSA_KV_HEADS * GROUP_W), rc, rs1, rs2).astype(BF16)
    kw_ref[0] = _rope(seg(OFF_KW, NSA_KV_HEADS * GROUP_W), rc, rs1, rs2).astype(BF16)
    vs_ref[0] = seg(OFF_VS, NSA_KV_HEADS * LANES).astype(BF16)
    vw_ref[0] = seg(OFF_VW, NSA_KV_HEADS * LANES).astype(BF16)
    kvc_ref[0, 0] = seg(OFF_KC, KV_W)
    kvc_ref[1, 0] = seg(OFF_VC, KV_W)
    gb_ref[0] = seg(OFF_GB, LANES)
    gate_ref[0, :, 0:D_MODEL] = seg(OFF_GATE, D_MODEL)
    gate_ref[0, :, D_MODEL:2 * D_MODEL] = seg(OFF_GATE + D_MODEL, D_MODEL)


def _proj(x, g, wp, bf, pq, pk, rc, rs1, rs2):
    B, T, _ = x.shape
    tm = PROJ_TM
    row = lambda w: pl.BlockSpec((1, tm, w), lambda b, t: (b, t, 0))
    const = lambda shape: pl.BlockSpec(shape, lambda b, t: (0,) * len(shape))
    tab = pl.BlockSpec((tm, LANES), lambda b, t: (t, 0))
    out_shapes = [
        jax.ShapeDtypeStruct((B, T, FOX_PAD_W), BF16),
        jax.ShapeDtypeStruct((B, T, FOX_PAD_W), BF16),
        jax.ShapeDtypeStruct((B, T, FOX_PAD_W), BF16),
        jax.ShapeDtypeStruct((B, T, NSA_W), BF16),
        jax.ShapeDtypeStruct((B, T, NSA_KV_HEADS * GROUP_W), BF16),
        jax.ShapeDtypeStruct((B, T, NSA_KV_HEADS * GROUP_W), BF16),
        jax.ShapeDtypeStruct((B, T, NSA_KV_HEADS * LANES), BF16),
        jax.ShapeDtypeStruct((B, T, NSA_KV_HEADS * LANES), BF16),
        jax.ShapeDtypeStruct((2, B, T, KV_W), F32),
        jax.ShapeDtypeStruct((B, T, LANES), F32),
        jax.ShapeDtypeStruct((B, T, 2 * D_MODEL), F32),
    ]
    out_specs = [
        row(FOX_PAD_W), row(FOX_PAD_W), row(FOX_PAD_W), row(NSA_W),
        row(NSA_KV_HEADS * GROUP_W), row(NSA_KV_HEADS * GROUP_W),
        row(NSA_KV_HEADS * LANES), row(NSA_KV_HEADS * LANES),
        pl.BlockSpec((2, 1, tm, KV_W), lambda b, t: (0, b, t, 0)),
        row(LANES), row(2 * D_MODEL),
    ]
    return pl.pallas_call(
        functools.partial(_proj_kernel, tm=tm),
        grid=(B, T // tm),
        in_specs=[row(D_MODEL), const((1, D_MODEL)), const((D_MODEL, PROJ_W)), const((1, LANES)),
                  const((3 * LANES, FOX_PAD_W)), const((3 * LANES, FOX_PAD_W)), tab, tab, tab],
        out_specs=out_specs,
        out_shape=out_shapes,
        scratch_shapes=[pltpu.VMEM((tm, D_MODEL), BF16), pltpu.VMEM((8, LANES), F32)],
        compiler_params=_cparams(("arbitrary", "arbitrary")),
        name="proj",
    )(x, g, wp, bf, pq, pk, rc, rs1, rs2)


def _online_update(s, v, m_ref, l_ref, acc_ref):
    tk = s.shape[1]
    m_prev = m_ref[...]
    m_new = jnp.maximum(m_prev, jnp.max(s, axis=1, keepdims=True))
    alpha = jnp.exp(m_prev - m_new)
    p = jnp.exp(s - _lane_tile(m_new, tk // LANES))
    l_ref[...] = alpha * l_ref[...] + jnp.sum(p, axis=1, keepdims=True)
    acc_ref[...] = alpha * acc_ref[...] + _dot(p.astype(BF16), v)
    m_ref[...] = m_new


def _fox_kernel(q_ref, k_ref, v_ref, o_ref, m_scr, l_scr, acc_scr, *, tq, tk):
    i = pl.program_id(2)
    q = q_ref[0]
    m_scr[...] = jnp.full_like(m_scr, NEG_BIG)
    l_scr[...] = jnp.zeros_like(l_scr)
    acc_scr[...] = jnp.zeros_like(acc_scr)

    def step(j, masked):
        start = pl.multiple_of(j * tk, tk)
        s = _dot_nt(q, k_ref[0, pl.ds(start, tk), :])
        if masked:
            qpos = i * tq + lax.broadcasted_iota(jnp.int32, (tq, tk), 0)
            kpos = j * tk + lax.broadcasted_iota(jnp.int32, (tq, tk), 1)
            s = jnp.where(kpos <= qpos, s, NEG_BIG)
        _online_update(s, v_ref[0, pl.ds(start, tk), :], m_scr, l_scr, acc_scr)

    n_full = (i * tq) // tk

    def body(j, carry):
        step(j, False)
        return carry

    lax.fori_loop(0, n_full, body, 0)
    step(n_full, True)
    o_ref[0] = (acc_scr[...] / l_scr[...]).astype(BF16)


def _fox(qa, ka, va):
    B, T, _ = qa.shape
    tq, tk = FOX_TQ, min(FOX_TK, T)
    return pl.pallas_call(
        functools.partial(_fox_kernel, tq=tq, tk=tk),
        grid=(B, FOX_HEADS, T // tq),
        in_specs=[pl.BlockSpec((1, tq, LANES), lambda b, h, i: (b, i, h)),
                  pl.BlockSpec((1, T, LANES), lambda b, h, i: (b, 0, h)),
                  pl.BlockSpec((1, T, LANES), lambda b, h, i: (b, 0, h))],
        out_specs=pl.BlockSpec((1, tq, LANES), lambda b, h, i: (b, i, h)),
        out_shape=jax.ShapeDtypeStruct((B, T, FOX_PAD_W), BF16),
        scratch_shapes=[pltpu.VMEM((tq, LANES), F32)] * 3,
        compiler_params=_cparams(("arbitrary", "arbitrary", "arbitrary")),
        name="fox",
    )(qa, ka, va)


def _cmp_kernel(r_ref, wt_ref, wb_ref, pt_ref, pb_ref, b1_ref, w2_ref, b2_ref, rc_ref, rs1_ref, rs2_ref, o_ref):
    rb = r_ref[0, 0]
    nr = rb.shape[0]
    a = _dot((rb + pt_ref[0]).astype(BF16), wt_ref[0])
    bm = _dot((rb + pb_ref[0]).astype(BF16), wb_ref[0])
    hdn = a + pltpu.roll(bm, nr - 1, 0) + b1_ref[0]
    act = 0.5 * hdn * (1.0 + jnp.tanh(0.7978845608028654 * (hdn + 0.044715 * (hdn * hdn * hdn))))
    y = _dot(act.astype(BF16), w2_ref[0]) + b2_ref[0]
    reps = y.shape[1] // LANES
    o_ref[0, 0] = _rope(y, _lane_tile(rc_ref[0], reps), _lane_tile(rs1_ref[0], reps),
                        _lane_tile(rs2_ref[0], reps)).astype(BF16)


def _cmp(kvc_rows, wt, wb, pt, pb, b1, w2, b2, rc, rs1, rs2):
    _, B, nr, rw = kvc_rows.shape
    ow = w2.shape[2]
    per_kv = lambda shape: pl.BlockSpec((1,) + shape, lambda s, b: (s,) + (0,) * len(shape))
    return pl.pallas_call(
        _cmp_kernel,
        grid=(2, B),
        in_specs=[pl.BlockSpec((1, 1, nr, rw), lambda s, b: (s, b, 0, 0)),
                  per_kv((rw, 2 * CMP_HIDDEN)), per_kv((rw, 2 * CMP_HIDDEN)),
                  per_kv((1, rw)), per_kv((1, rw)), per_kv((1, 2 * CMP_HIDDEN)),
                  per_kv((2 * CMP_HIDDEN, ow)), per_kv((1, ow)),
                  per_kv((nr, LANES)), per_kv((nr, LANES)), per_kv((nr, LANES))],
        out_specs=pl.BlockSpec((1, 1, nr, ow), lambda s, b: (s, b, 0, 0)),
        out_shape=jax.ShapeDtypeStruct((2, B, nr, ow), BF16),
        compiler_params=_cparams(("arbitrary", "arbitrary")),
        name="cmp",
    )(kvc_rows, wt, wb, pt, pb, b1, w2, b2, rc, rs1, rs2)


def _masked_heads(q):
    head = lax.broadcasted_iota(jnp.int32, q.shape, 1) >> 6
    return [jnp.where(head == r, q, jnp.zeros_like(q)) for r in range(NSA_GROUP)]


def _pack_heads(o):
    low = lax.broadcasted_iota(jnp.int32, o[0].shape, 1) < HEAD_DIM
    return jnp.concatenate([jnp.where(low, o[0], o[1]), jnp.where(low, o[2], o[3])], axis=1)


def _cmpsel_kernel(q_ref, kc_ref, vc_ref, map_ref, oc_ref, mb_ref, *, tq, n_sel, top):
    i = pl.program_id(2)
    nr = kc_ref.shape[2]
    kc = kc_ref[0, 0]
    vc = vc_ref[0, 0]
    qpos = i * tq + lax.broadcasted_iota(jnp.int32, (tq, nr), 0)
    cend = lax.broadcasted_iota(jnp.int32, (tq, nr), 1) * CMP_STRIDE + (CMP_BLOCK - 1)
    valid = cend <= qpos
    imp = jnp.zeros((tq, LANES), F32)
    outs = []
    for qm in _masked_heads(q_ref[0]):
        s = jnp.where(valid, _dot_nt(qm, kc), NEG_BIG)
        e = jnp.exp(s - jnp.max(s, axis=1, keepdims=True))
        p = jnp.where(valid, e / jnp.sum(e, axis=1, keepdims=True), 0.0).astype(BF16)
        outs.append(_dot(p, vc))
        imp = imp + _dot(p, map_ref[...])
    oc_ref[0] = _pack_heads(outs)

    sid = lax.broadcasted_iota(jnp.int32, (tq, LANES), 1)
    qp = i * tq + lax.broadcasted_iota(jnp.int32, (tq, LANES), 0)
    sidf = sid.astype(F32)
    score = jnp.where(sid * SEL_BLOCK > qp, -1.0,
                      jnp.where((sid == (qp >> 6)) | (sid == 0), FORCE_SCORE, imp))
    score = jnp.where(sid < n_sel, score, -3.0)
    maskb = jnp.full((tq, LANES), NEG_BIG, F32)
    for _ in range(top):
        mx = jnp.max(score, axis=1, keepdims=True)
        idx = jnp.min(jnp.where(score == mx, sidf, float(LANES)), axis=1, keepdims=True)
        pick = sidf == idx
        maskb = jnp.where(pick, 0.0, maskb)
        score = jnp.where(pick, -2.0, score)
    mb_ref[0, 0] = maskb.astype(BF16)


def _cmpsel(qb, cmpkv, selmap):
    B, T, _ = qb.shape
    nr = cmpkv.shape[2]
    tq = NSA_TQ
    n_sel = T // SEL_BLOCK
    return pl.pallas_call(
        functools.partial(_cmpsel_kernel, tq=tq, n_sel=n_sel, top=min(SEL_TOPK, n_sel)),
        grid=(B, NSA_KV_HEADS, T // tq),
        in_specs=[pl.BlockSpec((1, tq, GROUP_W), lambda b, g, i: (b, i, g)),
                  pl.BlockSpec((1, 1, nr, GROUP_W), lambda b, g, i: (0, b, 0, g)),
                  pl.BlockSpec((1, 1, nr, LANES), lambda b, g, i: (1, b, 0, 2 * g)),
                  pl.BlockSpec((nr, LANES), lambda b, g, i: (0, 0))],
        out_specs=[pl.BlockSpec((1, tq, GROUP_W), lambda b, g, i: (b, i, g)),
                   pl.BlockSpec((1, 1, tq, LANES), lambda b, g, i: (b, g, i, 0))],
        out_shape=[jax.ShapeDtypeStruct((B, T, NSA_W), F32),
                   jax.ShapeDtypeStruct((B, NSA_KV_HEADS, T, LANES), BF16)],
        compiler_params=_cparams(("arbitrary", "arbitrary", "arbitrary")),
        name="cmpsel",
    )(qb, cmpkv, cmpkv, selmap)


def _sel_kernel(q_ref, mb_ref, k_ref, v_ref, o_ref, qm_scr, m_scr, l_scr, acc_scr, *, tq, tk):
    i = pl.program_id(2)
    for r, qm in enumerate(_masked_heads(q_ref[0])):
        qm_scr[r] = qm
    m_scr[...] = jnp.full_like(m_scr, NEG_BIG)
    l_scr[...] = jnp.zeros_like(l_scr)
    acc_scr[...] = jnp.zeros_like(acc_scr)
    blocks_per_tile = tk // SEL_BLOCK

    def step(j, masked):
        start = pl.multiple_of(j * tk, tk)
        k = k_ref[0, pl.ds(start, tk), :]
        v = v_ref[0, pl.ds(start, tk), :]
        blk = lax.broadcasted_iota(jnp.int32, (LANES, tk), 0) - j * blocks_per_tile
        key_blk = lax.broadcasted_iota(jnp.int32, (LANES, tk), 1) >> 6
        onehot = jnp.where(blk == key_blk, 1.0, 0.0).astype(BF16)
        bias = _dot(mb_ref[0, 0], onehot)
        if masked:
            qpos = i * tq + lax.broadcasted_iota(jnp.int32, (tq, tk), 0)
            kpos = j * tk + lax.broadcasted_iota(jnp.int32, (tq, tk), 1)
            bias = jnp.where(kpos <= qpos, bias, NEG_BIG)
        for r in range(NSA_GROUP):
            _online_update(_dot_nt(qm_scr[r], k) + bias, v, m_scr.at[r], l_scr.at[r], acc_scr.at[r])

    n_full = (i * tq) // tk

    def body(j, carry):
        step(j, False)
        return carry

    lax.fori_loop(0, n_full, body, 0)
    step(n_full, True)
    o_ref[0] = _pack_heads([acc_scr[r] / l_scr[r] for r in range(NSA_GROUP)])


def _sel(qb, maskb, ks, vs):
    B, T, _ = qb.shape
    tq, tk = NSA_TQ, min(SEL_TK, T)
    return pl.pallas_call(
        functools.partial(_sel_kernel, tq=tq, tk=tk),
        grid=(B, NSA_KV_HEADS, T // tq),
        in_specs=[pl.BlockSpec((1, tq, GROUP_W), lambda b, g, i: (b, i, g)),
                  pl.BlockSpec((1, 1, tq, LANES), lambda b, g, i: (b, g, i, 0)),
                  pl.BlockSpec((1, T, GROUP_W), lambda b, g, i: (b, 0, g)),
                  pl.BlockSpec((1, T, LANES), lambda b, g, i: (b, 0, g))],
        out_specs=pl.BlockSpec((1, tq, GROUP_W), lambda b, g, i: (b, i, g)),
        out_shape=jax.ShapeDtypeStruct((B, T, NSA_W), F32),
        scratch_shapes=[pltpu.VMEM((NSA_GROUP, tq, GROUP_W), BF16)]
        + [pltpu.VMEM((NSA_GROUP, tq, LANES), F32)] * 3,
        compiler_params=_cparams(("arbitrary", "arbitrary", "arbitrary")),
        name="sel",
    )(qb, maskb, ks, vs)


def _win_kernel(q_ref, k_ref, v_ref, o_ref, *, tq):
    i = pl.program_id(2)
    wlen = WINDOW + tq
    start = pl.multiple_of(jnp.maximum(i * tq - WINDOW, 0), tq)
    k = k_ref[0, pl.ds(start, wlen), :]
    v = v_ref[0, pl.ds(start, wlen), :]
    qpos = i * tq + lax.broadcasted_iota(jnp.int32, (tq, wlen), 0)
    kpos = start + lax.broadcasted_iota(jnp.int32, (tq, wlen), 1)
    diff = qpos - kpos
    valid = (diff >= 0) & (diff < WINDOW)
    outs = []
    for qm in _masked_heads(q_ref[0]):
        s = jnp.where(valid, _dot_nt(qm, k), NEG_BIG)
        e = jnp.exp(s - jnp.max(s, axis=1, keepdims=True))
        outs.append(_dot(e.astype(BF16), v) / jnp.sum(e, axis=1, keepdims=True))
    o_ref[0] = _pack_heads(outs)


def _win(qb, kw, vw):
    B, T, _ = qb.shape
    tq = NSA_TQ
    return pl.pallas_call(
        functools.partial(_win_kernel, tq=tq),
        grid=(B, NSA_KV_HEADS, T // tq),
        in_specs=[pl.BlockSpec((1, tq, GROUP_W), lambda b, g, i: (b, i, g)),
                  pl.BlockSpec((1, T, GROUP_W), lambda b, g, i: (b, 0, g)),
                  pl.BlockSpec((1, T, LANES), lambda b, g, i: (b, 0, g))],
        out_specs=pl.BlockSpec((1, tq, GROUP_W), lambda b, g, i: (b, i, g)),
        out_shape=jax.ShapeDtypeStruct((B, T, NSA_W), F32),
        compiler_params=_cparams(("arbitrary", "arbitrary", "arbitrary")),
        name="win",
    )(qb, kw, vw)


def _merge_kernel(x_ref, oa_ref, oc_ref, os_ref, ow_ref, gb_ref, gate_ref, wof_ref, won_ref, wout_ref, e_ref, o_ref):
    ya = _dot(oa_ref[...], wof_ref[...])
    sg = jnp.concatenate(_split3(_sigmoid(gb_ref[...])), axis=1)
    on = (_dot(sg, e_ref[0]) * oc_ref[...] + _dot(sg, e_ref[1]) * os_ref[...]
          + _dot(sg, e_ref[2]) * ow_ref[...])
    yb = _dot(on.astype(BF16), won_ref[...])
    mixed = _sigmoid(gate_ref[:, 0:D_MODEL]) * ya + _sigmoid(gate_ref[:, D_MODEL:2 * D_MODEL]) * yb
    o_ref[...] = x_ref[...] + _dot(mixed.astype(BF16), wout_ref[...])


def _merge(x, oa, oc, os_, ow, gb, gate, wof, won, wout, e):
    n = x.shape[0]
    tm = MERGE_TM
    row = lambda w: pl.BlockSpec((tm, w), lambda t: (t, 0))
    const = lambda shape: pl.BlockSpec(shape, lambda t: (0,) * len(shape))
    return pl.pallas_call(
        _merge_kernel,
        grid=(n // tm,),
        in_specs=[row(D_MODEL), row(FOX_PAD_W), row(NSA_W), row(NSA_W), row(NSA_W), row(LANES), row(2 * D_MODEL),
                  const((FOX_PAD_W, D_MODEL)), const((NSA_W, D_MODEL)), const((D_MODEL, D_MODEL)),
                  const((3, 3 * LANES, NSA_W))],
        out_specs=row(D_MODEL),
        out_shape=jax.ShapeDtypeStruct((n, D_MODEL), F32),
        compiler_params=_cparams(("arbitrary",)),
        name="merge",
    )(x, oa, oc, os_, ow, gb, gate, wof, won, wout, e)


def _mlp_kernel(x_ref, g_ref, wu_ref, wd_ref, gf_ref, o_ref, h_scr, *, final_norm):
    x = x_ref[...]
    h_scr[...] = _rmsnorm(x, g_ref[...]).astype(BF16)
    acc = None
    for c0 in range(0, D_FF, MLP_FF_CHUNK):
        u = jnp.maximum(_dot(h_scr[...], wu_ref[:, c0:c0 + MLP_FF_CHUNK]), 0.0)
        part = _dot((u * u).astype(BF16), wd_ref[c0:c0 + MLP_FF_CHUNK, :])
        acc = part if acc is None else acc + part
    y = x + acc
    o_ref[...] = _rmsnorm(y, gf_ref[...]) if final_norm else y


def _mlp(x, g, wu, wd, gf, final_norm):
    n = x.shape[0]
    tm = MLP_TM
    row = pl.BlockSpec((tm, D_MODEL), lambda t: (t, 0))
    const = lambda shape: pl.BlockSpec(shape, lambda t: (0,) * len(shape))
    return pl.pallas_call(
        functools.partial(_mlp_kernel, final_norm=final_norm),
        grid=(n // tm,),
        in_specs=[row, const((1, D_MODEL)), const((D_MODEL, D_FF)), const((D_FF, D_MODEL)), const((1, D_MODEL))],
        out_specs=row,
        out_shape=jax.ShapeDtypeStruct((n, D_MODEL), F32),
        scratch_shapes=[pltpu.VMEM((tm, D_MODEL), BF16)],
        compiler_params=_cparams(("arbitrary",)),
        name="mlp",
    )(x, g, wu, wd, gf)


def _rope_tables(pos):
    inv = ROPE_THETA ** (-jnp.arange(ROPE_HALF, dtype=F32) / ROPE_HALF)
    ang = pos.astype(F32)[:, None] * inv[None, :]
    cos, sin = jnp.cos(ang), jnp.sin(ang)
    n = pos.shape[0]
    rest = HEAD_DIM - ROPE_DIM
    rc = jnp.concatenate([cos, cos, jnp.ones((n, rest), F32)], axis=1)
    rs1 = jnp.concatenate([-sin, jnp.zeros((n, HEAD_DIM - ROPE_HALF), F32)], axis=1)
    rs2 = jnp.concatenate([jnp.zeros((n, ROPE_HALF), F32), sin, jnp.zeros((n, rest), F32)], axis=1)
    two = lambda t: jnp.concatenate([t, t], axis=1)
    return two(rc), two(rs1), two(rs2)


def _pad_heads(w, n_heads):
    k = w.shape[0]
    w = w.reshape(k, n_heads, HEAD_DIM)
    return jnp.concatenate([w, jnp.zeros_like(w)], axis=2).reshape(k, n_heads * LANES)


def _rep_groups(w, reps):
    k = w.shape[0]
    w = w.reshape(k, NSA_KV_HEADS, 1, HEAD_DIM)
    return jnp.broadcast_to(w, (k, NSA_KV_HEADS, reps, HEAD_DIM)).reshape(k, NSA_KV_HEADS * reps * HEAD_DIM)


def _pad_cols(w, width):
    return jnp.concatenate([w, jnp.zeros((w.shape[0], width - w.shape[1]), w.dtype)], axis=1)


def _proj_weight(w_in):
    o = np.cumsum([0, FOX_W, FOX_W, FOX_W, FOX_HEADS, NSA_W, 6 * KV_W, 3 * NSA_HEADS, D_MODEL, D_MODEL])
    cols = lambda j: w_in[:, o[j]:o[j + 1]]
    kv = cols(5)
    kvp = lambda j: kv[:, j * KV_W:(j + 1) * KV_W]
    parts = [
        _pad_heads(cols(0) * SCALE, FOX_HEADS), _pad_heads(cols(1), FOX_HEADS), _pad_heads(cols(2), FOX_HEADS),
        _pad_cols(cols(3), LANES),
        cols(4) * SCALE,
        kvp(0), kvp(1),
        _rep_groups(kvp(2), NSA_GROUP), _rep_groups(kvp(3), 2),
        _rep_groups(kvp(4), NSA_GROUP), _rep_groups(kvp(5), 2),
        _pad_cols(cols(6), LANES),
        cols(7), cols(8),
    ]
    return jnp.concatenate(parts, axis=1).astype(BF16)


def _aug_placement():
    pq = np.zeros((3 * LANES, FOX_PAD_W), np.float32)
    pk = np.zeros((3 * LANES, FOX_PAD_W), np.float32)
    for h in range(FOX_HEADS):
        for piece in range(3):
            pq[piece * LANES + h, h * LANES + AUG_Q_C + piece] = 1.0
            pk[piece * LANES + h, h * LANES + AUG_K_C + piece] = -1.0
    return jnp.asarray(pq, BF16), jnp.asarray(pk, BF16)


def _gate_placement():
    e = np.zeros((3, 3 * LANES, NSA_W), np.float32)
    for j in range(3):
        for h in range(NSA_HEADS):
            for piece in range(3):
                e[j, piece * LANES + 3 * h + j, h * HEAD_DIM:(h + 1) * HEAD_DIM] = 1.0
    return jnp.asarray(e, BF16)


def _sel_map(T, nr):
    nc = (T - CMP_BLOCK) // CMP_STRIDE + 1
    n_sel = T // SEL_BLOCK
    cs = np.arange(nc) * CMP_STRIDE
    ce = cs + CMP_BLOCK
    ss = np.arange(n_sel) * SEL_BLOCK
    se = ss + SEL_BLOCK
    ov = np.clip(np.minimum(ce[:, None], se[None, :]) - np.maximum(cs[:, None], ss[None, :]), 0, None)
    m = np.zeros((nr, LANES), np.float32)
    m[:nc, :n_sel] = ov / CMP_BLOCK
    return jnp.asarray(m, BF16)


def _cmp_weights(cmp_pos, cmp_w1, cmp_b1, cmp_w2, cmp_b2):
    half = CMP_BLOCK // 2
    g2 = NSA_KV_HEADS

    def first(w1):
        w = w1.reshape(2, half, HEAD_DIM, CMP_HIDDEN)
        out = jnp.zeros((2, half, g2, HEAD_DIM, g2, CMP_HIDDEN), F32)
        for g in range(g2):
            out = out.at[:, :, g, :, g, :].set(w)
        return out.reshape(2, half * g2 * HEAD_DIM, g2 * CMP_HIDDEN).astype(BF16)

    def pos(p):
        p = p.reshape(2, half, 1, HEAD_DIM)
        return jnp.broadcast_to(p, (2, half, g2, HEAD_DIM)).reshape(2, 1, half * g2 * HEAD_DIM)

    def second(w2):
        out = jnp.zeros((g2, CMP_HIDDEN, g2, NSA_GROUP, HEAD_DIM), F32)
        for g in range(g2):
            out = out.at[g, :, g, :, :].set(jnp.broadcast_to(w2[:, None, :], (CMP_HIDDEN, NSA_GROUP, HEAD_DIM)))
        return out.reshape(g2 * CMP_HIDDEN, g2 * GROUP_W).astype(BF16)

    w1s = [first(cmp_w1[s]) for s in range(2)]
    ps = [pos(cmp_pos[s]) for s in range(2)]
    wt = jnp.stack([w[0] for w in w1s])
    wb = jnp.stack([w[1] for w in w1s])
    pt = jnp.stack([p[0] for p in ps])
    pb = jnp.stack([p[1] for p in ps])
    b1 = jnp.stack([jnp.tile(cmp_b1[s], g2)[None, :] for s in range(2)])
    w2 = jnp.stack([second(cmp_w2[s]) for s in range(2)])
    b2 = jnp.stack([jnp.tile(cmp_b2[s], g2 * NSA_GROUP)[None, :] for s in range(2)])
    return wt, wb, pt, pb, b1, w2, b2


def _layer(x, norm_mix, w_in, b_forget, cmp_pos, cmp_w1, cmp_b1, cmp_w2, cmp_b2,
           w_o_fox, w_o_nsa, w_out, norm_mlp, w_up, w_down, norm_final, final_norm):
    B, T, _ = x.shape
    nr = T // CMP_STRIDE
    pq, pk = _aug_placement()
    rc, rs1, rs2 = _rope_tables(jnp.arange(T))
    bf = _pad_cols(b_forget[None, :], LANES)
    qa, ka, va, qb, ks, kw, vs, vw, kvc, gb, gate = _proj(
        x, norm_mix[None, :], _proj_weight(w_in), bf, pq, pk, rc, rs1, rs2)

    oa = _fox(qa, ka, va)

    crc, crs1, crs2 = _rope_tables(jnp.arange(nr) * CMP_STRIDE + CMP_BLOCK - 1)
    ident = lambda t, one: jnp.stack([t, jnp.ones_like(t) if one else jnp.zeros_like(t)])
    cmpkv = _cmp(kvc.reshape(2, B, nr, CMP_STRIDE * KV_W), *_cmp_weights(cmp_pos, cmp_w1, cmp_b1, cmp_w2, cmp_b2),
                 ident(crc, True), ident(crs1, False), ident(crs2, False))

    oc, maskb = _cmpsel(qb, cmpkv, _sel_map(T, nr))
    os_ = _sel(qb, maskb, ks, vs)
    ow = _win(qb, kw, vw)

    n = B * T
    flat = lambda a: a.reshape(n, a.shape[-1])
    wof = jnp.concatenate([w_o_fox.reshape(FOX_HEADS, HEAD_DIM, D_MODEL),
                           jnp.zeros((FOX_HEADS, LANES - HEAD_DIM, D_MODEL), F32)], axis=1)
    x1 = _merge(flat(x), flat(oa), flat(oc), flat(os_), flat(ow), flat(gb), flat(gate),
                wof.reshape(FOX_PAD_W, D_MODEL).astype(BF16), w_o_nsa.astype(BF16), w_out.astype(BF16),
                _gate_placement())
    x2 = _mlp(x1, norm_mlp[None, :], w_up.astype(BF16), w_down.astype(BF16), norm_final[None, :], final_norm)
    return x2.reshape(B, T, D_MODEL)


def kernel(x, norm_mix, w_in, b_forget, cmp_pos, cmp_w1, cmp_b1, cmp_w2, cmp_b2, w_o_fox, w_o_nsa, w_out,
           norm_mlp, w_up, w_down, norm_final):
    depth = w_in.shape[0]
    for l in range(depth):
        x = _layer(x, norm_mix[l], w_in[l], b_forget[l], cmp_pos[l], cmp_w1[l], cmp_b1[l], cmp_w2[l], cmp_b2[l],
                   w_o_fox[l], w_o_nsa[l], w_out[l], norm_mlp[l], w_up[l], w_down[l], norm_final,
                   final_norm=(l == depth - 1))
    return x
```

```python
import functools

import numpy as np
import jax
import jax.numpy as jnp
from jax import lax
from jax.experimental import pallas as pl
from jax.experimental.pallas import tpu as pltpu

F32 = jnp.float32
BF16 = jnp.bfloat16

D_MODEL = 1024
HEAD_DIM = 64
FOX_HEADS = 8
NSA_HEADS = 8
NSA_KV_HEADS = 2
NSA_GROUP = NSA_HEADS // NSA_KV_HEADS
CMP_BLOCK = 32
CMP_STRIDE = 16
CMP_HIDDEN = 4 * HEAD_DIM
SEL_BLOCK = 64
SEL_TOPK = 16
WINDOW = 512
ROPE_THETA = 500000.0
ROPE_DIM = HEAD_DIM // 4
ROPE_HALF = ROPE_DIM // 2
D_FF = 4 * D_MODEL
RMS_EPS = 1e-6
FORCE_SCORE = 1e6
NEG_BIG = -1e30
SCALE = HEAD_DIM ** -0.5
LOG2E = 1.4426950408889634

FOX_W = FOX_HEADS * HEAD_DIM
NSA_W = NSA_HEADS * HEAD_DIM
KV_W = NSA_KV_HEADS * HEAD_DIM

LANES = 128
VMEM_LIMIT_BYTES = 56 * 1024 * 1024

FOX_PAD_W = FOX_HEADS * LANES
GROUP_W = NSA_GROUP * HEAD_DIM
OFF_QA = 0
OFF_KA = OFF_QA + FOX_PAD_W
OFF_VA = OFF_KA + FOX_PAD_W
OFF_F = OFF_VA + FOX_W
OFF_QB = OFF_F + LANES
OFF_KC = OFF_QB + NSA_W
OFF_VC = OFF_KC + KV_W
OFF_KS = OFF_VC + KV_W
OFF_VS = OFF_KS + KV_W
OFF_KW = OFF_VS + KV_W
OFF_VW = OFF_KW + KV_W
OFF_GB = OFF_VW + KV_W
OFF_GATE = OFF_GB + LANES
PROJ_W = OFF_GATE + 2 * D_MODEL

AUG_Q_C = HEAD_DIM
AUG_K_C = HEAD_DIM + 3

PROJ_TM = 256
FOX_TQ = 512
FOX_TK = 512
FOX_HEADS_PER_STEP = 2
NSA_TQ = 256
SEL_TK = 512
MERGE_TM = 256
MLP_TM = 512
MLP_FF_CHUNK = 512


def _cparams(sem):
    return pltpu.CompilerParams(dimension_semantics=sem, vmem_limit_bytes=VMEM_LIMIT_BYTES)


def _dot(a, b):
    return jnp.dot(a, b, preferred_element_type=F32)


def _split3(x):
    hi = x.astype(BF16)
    r = x - hi.astype(F32)
    mid = r.astype(BF16)
    lo = (r - mid.astype(F32)).astype(BF16)
    return hi, mid, lo


def _lane_tile(x, reps):
    return x if reps == 1 else jnp.concatenate([x] * reps, axis=1)


def _rope(y, rc, rs1, rs2):
    n = y.shape[1]
    return y * rc + pltpu.roll(y, n - ROPE_HALF, 1) * rs1 + pltpu.roll(y, ROPE_HALF, 1) * rs2


def _rmsnorm(x, g):
    return x * lax.rsqrt(jnp.mean(x * x, axis=-1, keepdims=True) + RMS_EPS) * g


def _sigmoid(x):
    return 1.0 / (1.0 + jnp.exp(-x))


def _proj_kernel(x_ref, g_ref, w_ref, bf_ref, pq_ref, pk_ref, rc_ref, rs1_ref, rs2_ref,
                 qa_ref, ka_ref, va_ref, qb_ref, ks_ref, kw_ref, vs_ref, vw_ref, kvc_ref, gb_ref, gate_ref,
                 h_scr, carry_scr, *, tm):
    t = pl.program_id(1)

    @pl.when(t == 0)
    def _():
        carry_scr[...] = jnp.zeros_like(carry_scr)

    h_scr[...] = _rmsnorm(x_ref[0], g_ref[...]).astype(BF16)

    def seg(c0, n):
        return _dot(h_scr[...], w_ref[:, c0:c0 + n])

    f = seg(OFF_F, LANES) + bf_ref[...]
    lf = jnp.minimum(f, 0.0) - jnp.log1p(jnp.exp(-jnp.abs(f)))
    row = lax.broadcasted_iota(jnp.int32, (tm, tm), 0)
    col = lax.broadcasted_iota(jnp.int32, (tm, tm), 1)
    tri = jnp.where(col <= row, 1.0, 0.0).astype(BF16)
    hi, mid, lo = _split3(lf)
    c = _dot(tri, hi) + _dot(tri, mid) + _dot(tri, lo) + carry_scr[0:1, :]
    carry_scr[0:1, :] = c[tm - 1:tm, :]
    c3 = jnp.concatenate(_split3(c * LOG2E), axis=1)

    lane = lax.broadcasted_iota(jnp.int32, (tm, FOX_PAD_W), 1) & (LANES - 1)
    ones_q = jnp.where((lane >= AUG_K_C) & (lane < AUG_K_C + 3), 1.0, 0.0)
    ones_k = jnp.where((lane >= AUG_Q_C) & (lane < AUG_Q_C + 3), 1.0, 0.0)
    qa_ref[0] = (seg(OFF_QA, FOX_PAD_W) + _dot(c3, pq_ref[...]) + ones_q).T.astype(BF16)
    ka_ref[0] = (seg(OFF_KA, FOX_PAD_W) + _dot(c3, pk_ref[...]) + ones_k).astype(BF16)
    va_ref[0] = seg(OFF_VA, FOX_W).T.astype(BF16)

    rc, rs1, rs2 = rc_ref[...], rs1_ref[...], rs2_ref[...]
    reps = NSA_W // LANES
    qb_ref[0] = _rope(seg(OFF_QB, NSA_W), _lane_tile(rc, reps), _lane_tile(rs1, reps),
                      _lane_tile(rs2, reps)).T.astype(BF16)
    pos = t * tm + lax.broadcasted_iota(jnp.int32, (tm, LANES), 0)
    onehot = jnp.where((pos >> 6) == lax.broadcasted_iota(jnp.int32, (tm, LANES), 1), 1.0, 0.0)
    ks_ref[0] = jnp.concatenate([_rope(seg(OFF_KS, KV_W), rc, rs1, rs2), onehot], axis=1).astype(BF16)
    kw_ref[0] = _rope(seg(OFF_KW, KV_W), rc, rs1, rs2).astype(BF16)
    vs_ref[0] = seg(OFF_VS, KV_W).T.astype(BF16)
    vw_ref[0] = seg(OFF_VW, KV_W).T.astype(BF16)
    kvc_ref[0, 0] = seg(OFF_KC, KV_W)
    kvc_ref[1, 0] = seg(OFF_VC, KV_W)
    gb_ref[0] = seg(OFF_GB, LANES)
    gate_ref[0, :, 0:D_MODEL] = seg(OFF_GATE, D_MODEL)
    gate_ref[0, :, D_MODEL:2 * D_MODEL] = seg(OFF_GATE + D_MODEL, D_MODEL)


def _proj(x, g, wp, bf, pq, pk, rc, rs1, rs2):
    B, T, _ = x.shape
    tm = PROJ_TM
    row = lambda w: pl.BlockSpec((1, tm, w), lambda b, t: (b, t, 0))
    col = lambda w: pl.BlockSpec((1, w, tm), lambda b, t: (b, 0, t))
    const = lambda shape: pl.BlockSpec(shape, lambda b, t: (0,) * len(shape))
    tab = pl.BlockSpec((tm, LANES), lambda b, t: (t, 0))
    out_shapes = [
        jax.ShapeDtypeStruct((B, FOX_PAD_W, T), BF16),
        jax.ShapeDtypeStruct((B, T, FOX_PAD_W), BF16),
        jax.ShapeDtypeStruct((B, FOX_W, T), BF16),
        jax.ShapeDtypeStruct((B, NSA_W, T), BF16),
        jax.ShapeDtypeStruct((B, T, KV_W + LANES), BF16),
        jax.ShapeDtypeStruct((B, T, KV_W), BF16),
        jax.ShapeDtypeStruct((B, KV_W, T), BF16),
        jax.ShapeDtypeStruct((B, KV_W, T), BF16),
        jax.ShapeDtypeStruct((2, B, T, KV_W), F32),
        jax.ShapeDtypeStruct((B, T, LANES), F32),
        jax.ShapeDtypeStruct((B, T, 2 * D_MODEL), F32),
    ]
    out_specs = [
        col(FOX_PAD_W), row(FOX_PAD_W), col(FOX_W), col(NSA_W),
        row(KV_W + LANES), row(KV_W), col(KV_W), col(KV_W),
        pl.BlockSpec((2, 1, tm, KV_W), lambda b, t: (0, b, t, 0)),
        row(LANES), row(2 * D_MODEL),
    ]
    return pl.pallas_call(
        functools.partial(_proj_kernel, tm=tm),
        grid=(B, T // tm),
        in_specs=[row(D_MODEL), const((1, D_MODEL)), const((D_MODEL, PROJ_W)), const((1, LANES)),
                  const((3 * LANES, FOX_PAD_W)), const((3 * LANES, FOX_PAD_W)), tab, tab, tab],
        out_specs=out_specs,
        out_shape=out_shapes,
        scratch_shapes=[pltpu.VMEM((tm, D_MODEL), BF16), pltpu.VMEM((8, LANES), F32)],
        compiler_params=_cparams(("arbitrary", "arbitrary")),
        name="proj",
    )(x, g, wp, bf, pq, pk, rc, rs1, rs2)


def _flash_step_t(st, vt, carry):
    m, l, acc = carry
    m_new = jnp.maximum(m, jnp.max(st, axis=0, keepdims=True))
    alpha = jnp.exp2(m - m_new)
    p = jnp.exp2(st - m_new)
    l = alpha * l + jnp.sum(p, axis=0, keepdims=True)
    acc = alpha * acc + _dot(vt, p.astype(BF16))
    return m_new, l, acc


def _causal_mask_t(st, i, j, tq, tk):
    kpos = j * tk + lax.broadcasted_iota(jnp.int32, (tk, tq), 0)
    qpos = i * tq + lax.broadcasted_iota(jnp.int32, (tk, tq), 1)
    return jnp.where(kpos <= qpos, st, NEG_BIG)


def _causal_flash_t(i, n_heads, tq, tk, score_fn, vt_fn, s0_scr, s1_scr, m_scr, l_scr, acc_scr):
    n_full = (i * tq) // tk

    def scores(j, s_scr):
        start = pl.multiple_of(j * tk, tk)
        for h in range(n_heads):
            s_scr[h] = score_fn(h, start)

    def softmax_pv(j, s_scr, masked):
        start = pl.multiple_of(j * tk, tk)
        for h in range(n_heads):
            st = s_scr[h]
            if masked:
                st = _causal_mask_t(st, i, j, tq, tk)
            m, l, acc = _flash_step_t(st, vt_fn(h, start), (m_scr[h], l_scr[h], acc_scr[h]))
            m_scr[h] = m
            l_scr[h] = l
            acc_scr[h] = acc

    m_scr[...] = jnp.full_like(m_scr, NEG_BIG)
    l_scr[...] = jnp.zeros_like(l_scr)
    acc_scr[...] = jnp.zeros_like(acc_scr)

    odd = n_full & 1

    @pl.when(odd == 1)
    def _():
        scores(0, s1_scr)
        softmax_pv(0, s1_scr, False)

    scores(odd, s0_scr)

    def body(jp, carry):
        t0 = odd + 2 * jp
        scores(t0 + 1, s1_scr)
        softmax_pv(t0, s0_scr, False)
        scores(t0 + 2, s0_scr)
        softmax_pv(t0 + 1, s1_scr, False)
        return carry

    lax.fori_loop(0, n_full // 2, body, 0)
    softmax_pv(n_full, s0_scr, True)
    for h in range(n_heads):
        acc_scr[h] = acc_scr[h] / l_scr[h]


def _flash_scratch(n_heads, tq, tk):
    return [pltpu.VMEM((n_heads, tk, tq), F32), pltpu.VMEM((n_heads, tk, tq), F32),
            pltpu.VMEM((n_heads, 1, tq), F32), pltpu.VMEM((n_heads, 1, tq), F32),
            pltpu.VMEM((n_heads, HEAD_DIM, tq), F32)]


def _fox_kernel(qt_ref, k_ref, vt_ref, o_ref, s0_scr, s1_scr, m_scr, l_scr, acc_scr, *, tq, tk, heads):
    def score_fn(h, start):
        return _dot(k_ref[0, pl.ds(start, tk), h * LANES:(h + 1) * LANES], qt_ref[0, h * LANES:(h + 1) * LANES, :])

    def vt_fn(h, start):
        return vt_ref[0, h * HEAD_DIM:(h + 1) * HEAD_DIM, pl.ds(start, tk)]

    _causal_flash_t(pl.program_id(2), heads, tq, tk, score_fn, vt_fn, s0_scr, s1_scr, m_scr, l_scr, acc_scr)
    o_ref[0] = acc_scr[...].reshape(heads * HEAD_DIM, tq).T.astype(BF16)


def _fox(qat, ka, vat):
    B, T, _ = ka.shape
    tq, tk = min(FOX_TQ, T), min(FOX_TK, T)
    heads = FOX_HEADS_PER_STEP
    return pl.pallas_call(
        functools.partial(_fox_kernel, tq=tq, tk=tk, heads=heads),
        grid=(B, FOX_HEADS // heads, T // tq),
        in_specs=[pl.BlockSpec((1, heads * LANES, tq), lambda b, h, i: (b, h, i)),
                  pl.BlockSpec((1, T, heads * LANES), lambda b, h, i: (b, 0, h)),
                  pl.BlockSpec((1, heads * HEAD_DIM, T), lambda b, h, i: (b, h, 0))],
        out_specs=pl.BlockSpec((1, tq, heads * HEAD_DIM), lambda b, h, i: (b, i, h)),
        out_shape=jax.ShapeDtypeStruct((B, T, FOX_W), BF16),
        scratch_shapes=_flash_scratch(heads, tq, tk),
        compiler_params=_cparams(("arbitrary", "arbitrary", "arbitrary")),
        name="fox",
    )(qat, ka, vat)


def _cmp_kernel(r_ref, wt_ref, wb_ref, pt_ref, pb_ref, b1_ref, w2_ref, b2_ref, rc_ref, rs1_ref, rs2_ref,
                o_ref, ot_ref):
    rb = r_ref[0, 0]
    nr = rb.shape[0]
    a = _dot((rb + pt_ref[0]).astype(BF16), wt_ref[0])
    bm = _dot((rb + pb_ref[0]).astype(BF16), wb_ref[0])
    hdn = a + pltpu.roll(bm, nr - 1, 0) + b1_ref[0]
    act = 0.5 * hdn * (1.0 + jnp.tanh(0.7978845608028654 * (hdn + 0.044715 * (hdn * hdn * hdn))))
    y = _rope(_dot(act.astype(BF16), w2_ref[0]) + b2_ref[0], rc_ref[0], rs1_ref[0], rs2_ref[0])
    o_ref[0, 0] = y.astype(BF16)
    ot_ref[0, 0] = y.T.astype(BF16)


def _cmp(kvc_rows, wt, wb, pt, pb, b1, w2, b2, rc, rs1, rs2):
    _, B, nr, rw = kvc_rows.shape
    per_kv = lambda shape: pl.BlockSpec((1,) + shape, lambda s, b: (s,) + (0,) * len(shape))
    return pl.pallas_call(
        _cmp_kernel,
        grid=(2, B),
        in_specs=[pl.BlockSpec((1, 1, nr, rw), lambda s, b: (s, b, 0, 0)),
                  per_kv((rw, 2 * CMP_HIDDEN)), per_kv((rw, 2 * CMP_HIDDEN)),
                  per_kv((1, rw)), per_kv((1, rw)), per_kv((1, 2 * CMP_HIDDEN)),
                  per_kv((2 * CMP_HIDDEN, KV_W)), per_kv((1, KV_W)),
                  per_kv((nr, LANES)), per_kv((nr, LANES)), per_kv((nr, LANES))],
        out_specs=[pl.BlockSpec((1, 1, nr, KV_W), lambda s, b: (s, b, 0, 0)),
                   pl.BlockSpec((1, 1, KV_W, nr), lambda s, b: (s, b, 0, 0))],
        out_shape=[jax.ShapeDtypeStruct((2, B, nr, KV_W), BF16), jax.ShapeDtypeStruct((2, B, KV_W, nr), BF16)],
        compiler_params=_cparams(("arbitrary", "arbitrary")),
        name="cmp",
    )(kvc_rows, wt, wb, pt, pb, b1, w2, b2, rc, rs1, rs2)


def _group_query_weights(qt, g):
    zero = jnp.zeros((HEAD_DIM, qt.shape[1]), qt.dtype)
    out = []
    for r in range(NSA_GROUP):
        q = qt[r * HEAD_DIM:(r + 1) * HEAD_DIM, :]
        out.append(jnp.concatenate([jnp.where(g == 0, q, zero), jnp.where(g == 1, q, zero)], axis=0))
    return out


def _cmpsel_kernel(qt_ref, kc_ref, vct_ref, mapt_ref, oc_ref, mt_ref, *, tq, n_sel, top):
    g = pl.program_id(1)
    i = pl.program_id(2)
    nr = kc_ref.shape[2]
    kc = kc_ref[0, 0]
    vct = vct_ref[0, 0]
    cend = lax.broadcasted_iota(jnp.int32, (nr, tq), 0) * CMP_STRIDE + (CMP_BLOCK - 1)
    qpos = i * tq + lax.broadcasted_iota(jnp.int32, (nr, tq), 1)
    valid = cend <= qpos
    sts = [_dot(kc, w) for w in _group_query_weights(qt_ref[0], g)]
    imp = jnp.zeros((LANES, tq), F32)
    outs = []
    for st in sts:
        st = jnp.where(valid, st, NEG_BIG)
        e = jnp.exp2(st - jnp.max(st, axis=0, keepdims=True))
        p = jnp.where(valid, e / jnp.sum(e, axis=0, keepdims=True), 0.0).astype(BF16)
        outs.append(_dot(vct, p))
        imp = imp + _dot(mapt_ref[...], p)
    oc_ref[0] = jnp.concatenate(outs, axis=0).T

    sid = lax.broadcasted_iota(jnp.int32, (LANES, tq), 0)
    qp = i * tq + lax.broadcasted_iota(jnp.int32, (LANES, tq), 1)
    sidf = sid.astype(F32)
    score = jnp.where(sid * SEL_BLOCK > qp, -1.0,
                      jnp.where((sid == (qp >> 6)) | (sid == 0), FORCE_SCORE, imp))
    score = jnp.where(sid < n_sel, score, -3.0)
    maskt = jnp.full((LANES, tq), NEG_BIG, F32)
    for _ in range(top):
        mx = jnp.max(score, axis=0, keepdims=True)
        idx = jnp.min(jnp.where(score == mx, sidf, float(LANES)), axis=0, keepdims=True)
        pick = sidf == idx
        maskt = jnp.where(pick, 0.0, maskt)
        score = jnp.where(pick, -2.0, score)
    mt_ref[0, 0] = maskt.astype(BF16)


def _cmpsel(qbt, kc, vct, selmap_t):
    B, _, T = qbt.shape
    nr = kc.shape[2]
    tq = NSA_TQ
    n_sel = T // SEL_BLOCK
    return pl.pallas_call(
        functools.partial(_cmpsel_kernel, tq=tq, n_sel=n_sel, top=min(SEL_TOPK, n_sel)),
        grid=(B, NSA_KV_HEADS, T // tq),
        in_specs=[pl.BlockSpec((1, GROUP_W, tq), lambda b, g, i: (b, g, i)),
                  pl.BlockSpec((1, 1, nr, KV_W), lambda b, g, i: (0, b, 0, 0)),
                  pl.BlockSpec((1, 1, HEAD_DIM, nr), lambda b, g, i: (1, b, g, 0)),
                  pl.BlockSpec((LANES, nr), lambda b, g, i: (0, 0))],
        out_specs=[pl.BlockSpec((1, tq, GROUP_W), lambda b, g, i: (b, i, g)),
                   pl.BlockSpec((1, 1, LANES, tq), lambda b, g, i: (b, g, 0, i))],
        out_shape=[jax.ShapeDtypeStruct((B, T, NSA_W), F32),
                   jax.ShapeDtypeStruct((B, NSA_KV_HEADS, LANES, T), BF16)],
        compiler_params=_cparams(("arbitrary", "arbitrary", "arbitrary")),
        name="cmpsel",
    )(qbt, kc, vct, selmap_t)


def _sel_kernel(qt_ref, mt_ref, k_ref, vt_ref, o_ref, w_scr, s0_scr, s1_scr, m_scr, l_scr, acc_scr, *, tq, tk):
    g = pl.program_id(1)
    for r, w in enumerate(_group_query_weights(qt_ref[0], g)):
        w_scr[r] = jnp.concatenate([w, mt_ref[0, 0]], axis=0)

    def score_fn(r, start):
        return _dot(k_ref[0, pl.ds(start, tk), :], w_scr[r])

    def vt_fn(r, start):
        return vt_ref[0, :, pl.ds(start, tk)]

    _causal_flash_t(pl.program_id(2), NSA_GROUP, tq, tk, score_fn, vt_fn, s0_scr, s1_scr, m_scr, l_scr, acc_scr)
    o_ref[0] = acc_scr[...].reshape(GROUP_W, tq).T


def _sel(qbt, maskt, ks, vst):
    B, _, T = qbt.shape
    tq, tk = NSA_TQ, min(SEL_TK, T)
    kw = KV_W + LANES
    return pl.pallas_call(
        functools.partial(_sel_kernel, tq=tq, tk=tk),
        grid=(B, NSA_KV_HEADS, T // tq),
        in_specs=[pl.BlockSpec((1, GROUP_W, tq), lambda b, g, i: (b, g, i)),
                  pl.BlockSpec((1, 1, LANES, tq), lambda b, g, i: (b, g, 0, i)),
                  pl.BlockSpec((1, T, kw), lambda b, g, i: (b, 0, 0)),
                  pl.BlockSpec((1, HEAD_DIM, T), lambda b, g, i: (b, g, 0))],
        out_specs=pl.BlockSpec((1, tq, GROUP_W), lambda b, g, i: (b, i, g)),
        out_shape=jax.ShapeDtypeStruct((B, T, NSA_W), F32),
        scratch_shapes=[pltpu.VMEM((NSA_GROUP, kw, tq), BF16)] + _flash_scratch(NSA_GROUP, tq, tk),
        compiler_params=_cparams(("arbitrary", "arbitrary", "arbitrary")),
        name="sel",
    )(qbt, maskt, ks, vst)


def _win_kernel(qt_ref, k_ref, vt_ref, o_ref, *, tq):
    g = pl.program_id(1)
    i = pl.program_id(2)
    wlen = WINDOW + tq
    start = pl.multiple_of(jnp.maximum(i * tq - WINDOW, 0), tq)
    k = k_ref[0, pl.ds(start, wlen), :]
    vt = vt_ref[0, :, pl.ds(start, wlen)]
    kpos = start + lax.broadcasted_iota(jnp.int32, (wlen, tq), 0)
    qpos = i * tq + lax.broadcasted_iota(jnp.int32, (wlen, tq), 1)
    diff = qpos - kpos
    valid = (diff >= 0) & (diff < WINDOW)
    sts = [_dot(k, w) for w in _group_query_weights(qt_ref[0], g)]
    outs = []
    for st in sts:
        st = jnp.where(valid, st, NEG_BIG)
        e = jnp.exp2(st - jnp.max(st, axis=0, keepdims=True))
        outs.append(_dot(vt, e.astype(BF16)) / jnp.sum(e, axis=0, keepdims=True))
    o_ref[0] = jnp.concatenate(outs, axis=0).T


def _win(qbt, kw, vwt):
    B, _, T = qbt.shape
    tq = NSA_TQ
    return pl.pallas_call(
        functools.partial(_win_kernel, tq=tq),
        grid=(B, NSA_KV_HEADS, T // tq),
        in_specs=[pl.BlockSpec((1, GROUP_W, tq), lambda b, g, i: (b, g, i)),
                  pl.BlockSpec((1, T, KV_W), lambda b, g, i: (b, 0, 0)),
                  pl.BlockSpec((1, HEAD_DIM, T), lambda b, g, i: (b, g, 0))],
        out_specs=pl.BlockSpec((1, tq, GROUP_W), lambda b, g, i: (b, i, g)),
        out_shape=jax.ShapeDtypeStruct((B, T, NSA_W), F32),
        compiler_params=_cparams(("arbitrary", "arbitrary", "arbitrary")),
        name="win",
    )(qbt, kw, vwt)


def _merge_kernel(x_ref, oa_ref, oc_ref, os_ref, ow_ref, gb_ref, gate_ref, wof_ref, won_ref, wout_ref, e_ref, o_ref):
    ya = _dot(oa_ref[...], wof_ref[...])
    sg = jnp.concatenate(_split3(_sigmoid(gb_ref[...])), axis=1)
    on = (_dot(sg, e_ref[0]) * oc_ref[...] + _dot(sg, e_ref[1]) * os_ref[...]
          + _dot(sg, e_ref[2]) * ow_ref[...])
    yb = _dot(on.astype(BF16), won_ref[...])
    mixed = _sigmoid(gate_ref[:, 0:D_MODEL]) * ya + _sigmoid(gate_ref[:, D_MODEL:2 * D_MODEL]) * yb
    o_ref[...] = x_ref[...] + _dot(mixed.astype(BF16), wout_ref[...])


def _merge(x, oa, oc, os_, ow, gb, gate, wof, won, wout, e):
    n = x.shape[0]
    tm = MERGE_TM
    row = lambda w: pl.BlockSpec((tm, w), lambda t: (t, 0))
    const = lambda shape: pl.BlockSpec(shape, lambda t: (0,) * len(shape))
    return pl.pallas_call(
        _merge_kernel,
        grid=(n // tm,),
        in_specs=[row(D_MODEL), row(FOX_W), row(NSA_W), row(NSA_W), row(NSA_W), row(LANES), row(2 * D_MODEL),
                  const((FOX_W, D_MODEL)), const((NSA_W, D_MODEL)), const((D_MODEL, D_MODEL)),
                  const((3, 3 * LANES, NSA_W))],
        out_specs=row(D_MODEL),
        out_shape=jax.ShapeDtypeStruct((n, D_MODEL), F32),
        compiler_params=_cparams(("arbitrary",)),
        name="merge",
    )(x, oa, oc, os_, ow, gb, gate, wof, won, wout, e)


def _mlp_kernel(x_ref, g_ref, wu_ref, wd_ref, gf_ref, o_ref, h_scr, *, final_norm):
    x = x_ref[...]
    h_scr[...] = _rmsnorm(x, g_ref[...]).astype(BF16)
    acc = None
    for c0 in range(0, D_FF, MLP_FF_CHUNK):
        u = jnp.maximum(_dot(h_scr[...], wu_ref[:, c0:c0 + MLP_FF_CHUNK]), 0.0)
        part = _dot((u * u).astype(BF16), wd_ref[c0:c0 + MLP_FF_CHUNK, :])
        acc = part if acc is None else acc + part
    y = x + acc
    o_ref[...] = _rmsnorm(y, gf_ref[...]) if final_norm else y


def _mlp(x, g, wu, wd, gf, final_norm):
    n = x.shape[0]
    tm = MLP_TM
    row = pl.BlockSpec((tm, D_MODEL), lambda t: (t, 0))
    const = lambda shape: pl.BlockSpec(shape, lambda t: (0,) * len(shape))
    return pl.pallas_call(
        functools.partial(_mlp_kernel, final_norm=final_norm),
        grid=(n // tm,),
        in_specs=[row, const((1, D_MODEL)), const((D_MODEL, D_FF)), const((D_FF, D_MODEL)), const((1, D_MODEL))],
        out_specs=row,
        out_shape=jax.ShapeDtypeStruct((n, D_MODEL), F32),
        scratch_shapes=[pltpu.VMEM((tm, D_MODEL), BF16)],
        compiler_params=_cparams(("arbitrary",)),
        name="mlp",
    )(x, g, wu, wd, gf)


def _rope_tables(pos):
    inv = ROPE_THETA ** (-jnp.arange(ROPE_HALF, dtype=F32) / ROPE_HALF)
    ang = pos.astype(F32)[:, None] * inv[None, :]
    cos, sin = jnp.cos(ang), jnp.sin(ang)
    n = pos.shape[0]
    rest = HEAD_DIM - ROPE_DIM
    rc = jnp.concatenate([cos, cos, jnp.ones((n, rest), F32)], axis=1)
    rs1 = jnp.concatenate([-sin, jnp.zeros((n, HEAD_DIM - ROPE_HALF), F32)], axis=1)
    rs2 = jnp.concatenate([jnp.zeros((n, ROPE_HALF), F32), sin, jnp.zeros((n, rest), F32)], axis=1)
    two = lambda t: jnp.concatenate([t, t], axis=1)
    return two(rc), two(rs1), two(rs2)


def _pad_heads(w, n_heads):
    k = w.shape[0]
    w = w.reshape(k, n_heads, HEAD_DIM)
    return jnp.concatenate([w, jnp.zeros_like(w)], axis=2).reshape(k, n_heads * LANES)


def _pad_cols(w, width):
    return jnp.concatenate([w, jnp.zeros((w.shape[0], width - w.shape[1]), w.dtype)], axis=1)


def _proj_weight(w_in):
    o = np.cumsum([0, FOX_W, FOX_W, FOX_W, FOX_HEADS, NSA_W, 6 * KV_W, 3 * NSA_HEADS, D_MODEL, D_MODEL])
    cols = lambda j: w_in[:, o[j]:o[j + 1]]
    qscale = SCALE * LOG2E
    parts = [
        _pad_heads(cols(0) * qscale, FOX_HEADS), _pad_heads(cols(1), FOX_HEADS), cols(2),
        _pad_cols(cols(3), LANES),
        cols(4) * qscale,
        cols(5),
        _pad_cols(cols(6), LANES),
        cols(7), cols(8),
    ]
    return jnp.concatenate(parts, axis=1).astype(BF16)


def _aug_placement():
    pq = np.zeros((3 * LANES, FOX_PAD_W), np.float32)
    pk = np.zeros((3 * LANES, FOX_PAD_W), np.float32)
    for h in range(FOX_HEADS):
        for piece in range(3):
            pq[piece * LANES + h, h * LANES + AUG_Q_C + piece] = 1.0
            pk[piece * LANES + h, h * LANES + AUG_K_C + piece] = -1.0
    return jnp.asarray(pq, BF16), jnp.asarray(pk, BF16)


def _gate_placement():
    e = np.zeros((3, 3 * LANES, NSA_W), np.float32)
    for j in range(3):
        for h in range(NSA_HEADS):
            for piece in range(3):
                e[j, piece * LANES + 3 * h + j, h * HEAD_DIM:(h + 1) * HEAD_DIM] = 1.0
    return jnp.asarray(e, BF16)


def _sel_map_t(T, nr):
    nc = (T - CMP_BLOCK) // CMP_STRIDE + 1
    n_sel = T // SEL_BLOCK
    cs = np.arange(nc) * CMP_STRIDE
    ce = cs + CMP_BLOCK
    ss = np.arange(n_sel) * SEL_BLOCK
    se = ss + SEL_BLOCK
    ov = np.clip(np.minimum(ce[None, :], se[:, None]) - np.maximum(cs[None, :], ss[:, None]), 0, None)
    m = np.zeros((LANES, nr), np.float32)
    m[:n_sel, :nc] = ov / CMP_BLOCK
    return jnp.asarray(m, BF16)


def _cmp_weights(cmp_pos, cmp_w1, cmp_b1, cmp_w2, cmp_b2):
    half = CMP_BLOCK // 2
    g2 = NSA_KV_HEADS

    def first(w1):
        w = w1.reshape(2, half, HEAD_DIM, CMP_HIDDEN)
        out = jnp.zeros((2, half, g2, HEAD_DIM, g2, CMP_HIDDEN), F32)
        for g in range(g2):
            out = out.at[:, :, g, :, g, :].set(w)
        return out.reshape(2, half * g2 * HEAD_DIM, g2 * CMP_HIDDEN).astype(BF16)

    def pos(p):
        p = p.reshape(2, half, 1, HEAD_DIM)
        return jnp.broadcast_to(p, (2, half, g2, HEAD_DIM)).reshape(2, 1, half * g2 * HEAD_DIM)

    def second(w2):
        out = jnp.zeros((g2, CMP_HIDDEN, g2, HEAD_DIM), F32)
        for g in range(g2):
            out = out.at[g, :, g, :].set(w2)
        return out.reshape(g2 * CMP_HIDDEN, KV_W).astype(BF16)

    w1s = [first(cmp_w1[s]) for s in range(2)]
    ps = [pos(cmp_pos[s]) for s in range(2)]
    wt = jnp.stack([w[0] for w in w1s])
    wb = jnp.stack([w[1] for w in w1s])
    pt = jnp.stack([p[0] for p in ps])
    pb = jnp.stack([p[1] for p in ps])
    b1 = jnp.stack([jnp.tile(cmp_b1[s], g2)[None, :] for s in range(2)])
    w2 = jnp.stack([second(cmp_w2[s]) for s in range(2)])
    b2 = jnp.stack([jnp.tile(cmp_b2[s], g2)[None, :] for s in range(2)])
    return wt, wb, pt, pb, b1, w2, b2


def _layer(x, norm_mix, w_in, b_forget, cmp_pos, cmp_w1, cmp_b1, cmp_w2, cmp_b2,
           w_o_fox, w_o_nsa, w_out, norm_mlp, w_up, w_down, norm_final, final_norm):
    B, T, _ = x.shape
    nr = T // CMP_STRIDE
    pq, pk = _aug_placement()
    rc, rs1, rs2 = _rope_tables(jnp.arange(T))
    bf = _pad_cols(b_forget[None, :], LANES)
    qat, ka, vat, qbt, ks, kw, vst, vwt, kvc, gb, gate = _proj(
        x, norm_mix[None, :], _proj_weight(w_in), bf, pq, pk, rc, rs1, rs2)

    oa = _fox(qat, ka, vat)

    crc, crs1, crs2 = _rope_tables(jnp.arange(nr) * CMP_STRIDE + CMP_BLOCK - 1)
    ident = lambda t, one: jnp.stack([t, jnp.ones_like(t) if one else jnp.zeros_like(t)])
    cmp_rows, cmp_cols = _cmp(kvc.reshape(2, B, nr, CMP_STRIDE * KV_W),
                              *_cmp_weights(cmp_pos, cmp_w1, cmp_b1, cmp_w2, cmp_b2),
                              ident(crc, True), ident(crs1, False), ident(crs2, False))

    oc, maskt = _cmpsel(qbt, cmp_rows, cmp_cols, _sel_map_t(T, nr))
    os_ = _sel(qbt, maskt, ks, vst)
    ow = _win(qbt, kw, vwt)

    n = B * T
    flat = lambda a: a.reshape(n, a.shape[-1])
    x1 = _merge(flat(x), flat(oa), flat(oc), flat(os_), flat(ow), flat(gb), flat(gate),
                w_o_fox.astype(BF16), w_o_nsa.astype(BF16), w_out.astype(BF16), _gate_placement())
    x2 = _mlp(x1, norm_mlp[None, :], w_up.astype(BF16), w_down.astype(BF16), norm_final[None, :], final_norm)
    return x2.reshape(B, T, D_MODEL)


def kernel(x, norm_mix, w_in, b_forget, cmp_pos, cmp_w1, cmp_b1, cmp_w2, cmp_b2, w_o_fox, w_o_nsa, w_out,
           norm_mlp, w_up, w_down, norm_final):
    depth = w_in.shape[0]
    for l in range(depth):
        x = _layer(x, norm_mix[l], w_in[l], b_forget[l], cmp_pos[l], cmp_w1[l], cmp_b1[l], cmp_w2[l], cmp_b2[l],
                   w_o_fox[l], w_o_nsa[l], w_out[l], norm_mlp[l], w_up[l], w_down[l], norm_final,
                   final_norm=(l == depth - 1))
    return x
```

```python
import functools

import numpy as np
import jax
import jax.numpy as jnp
from jax import lax
from jax.experimental import pallas as pl
from jax.experimental.pallas import tpu as pltpu

F32 = jnp.float32
BF16 = jnp.bfloat16

D_MODEL = 1024
HEAD_DIM = 64
FOX_HEADS = 8
NSA_HEADS = 8
NSA_KV_HEADS = 2
NSA_GROUP = NSA_HEADS // NSA_KV_HEADS
CMP_BLOCK = 32
CMP_STRIDE = 16
CMP_HIDDEN = 4 * HEAD_DIM
SEL_BLOCK = 64
SEL_TOPK = 16
WINDOW = 512
ROPE_THETA = 500000.0
ROPE_DIM = HEAD_DIM // 4
ROPE_HALF = ROPE_DIM // 2
D_FF = 4 * D_MODEL
RMS_EPS = 1e-6
FORCE_SCORE = 1e6
NEG_BIG = -1e30
SCALE = HEAD_DIM ** -0.5
LOG2E = 1.4426950408889634

FOX_W = FOX_HEADS * HEAD_DIM
NSA_W = NSA_HEADS * HEAD_DIM
KV_W = NSA_KV_HEADS * HEAD_DIM

LANES = 128
DEN_ROWS = 16
VMEM_LIMIT_BYTES = 56 * 1024 * 1024

FOX_PAD_W = FOX_HEADS * LANES
GROUP_W = NSA_GROUP * HEAD_DIM
OFF_QA = 0
OFF_KA = OFF_QA + FOX_PAD_W
OFF_VA = OFF_KA + FOX_PAD_W
OFF_F = OFF_VA + FOX_W
OFF_QB = OFF_F + LANES
OFF_KC = OFF_QB + NSA_W
OFF_VC = OFF_KC + KV_W
OFF_KS = OFF_VC + KV_W
OFF_VS = OFF_KS + KV_W
OFF_KW = OFF_VS + KV_W
OFF_VW = OFF_KW + KV_W
OFF_GB = OFF_VW + KV_W
OFF_GATE = OFF_GB + LANES
PROJ_W = OFF_GATE + 2 * D_MODEL

AUG_Q_C = HEAD_DIM
AUG_K_C = HEAD_DIM + 3

PROJ_TM = 256
FOX_TQ = 512
FOX_TK = 512
FOX_HEADS_PER_STEP = 4
NSA_TQ = 256
SEL_TQ = 512
SEL_TK = 512
MERGE_TM = 256
MLP_TM = 512
MLP_FF_CHUNK = 512


def _cparams(sem):
    return pltpu.CompilerParams(dimension_semantics=sem, vmem_limit_bytes=VMEM_LIMIT_BYTES)


def _dot(a, b):
    return jnp.dot(a, b, preferred_element_type=F32)


def _split3(x):
    hi = x.astype(BF16)
    r = x - hi.astype(F32)
    mid = r.astype(BF16)
    lo = (r - mid.astype(F32)).astype(BF16)
    return hi, mid, lo


def _lane_tile(x, reps):
    return x if reps == 1 else jnp.concatenate([x] * reps, axis=1)


def _rope(y, rc, rs1, rs2):
    n = y.shape[1]
    return y * rc + pltpu.roll(y, n - ROPE_HALF, 1) * rs1 + pltpu.roll(y, ROPE_HALF, 1) * rs2


def _rmsnorm(x, g):
    return x * lax.rsqrt(jnp.mean(x * x, axis=-1, keepdims=True) + RMS_EPS) * g


def _sigmoid(x):
    return 1.0 / (1.0 + jnp.exp(-x))


def _proj_kernel(x_ref, g_ref, w_ref, bf_ref, pq_ref, pk_ref, rc_ref, rs1_ref, rs2_ref,
                 qa_ref, ka_ref, va_ref, qb_ref, ks_ref, kw_ref, vs_ref, vw_ref, kvc_ref, gb_ref, gate_ref,
                 h_scr, carry_scr, *, tm):
    t = pl.program_id(1)

    @pl.when(t == 0)
    def _():
        carry_scr[...] = jnp.zeros_like(carry_scr)

    h_scr[...] = _rmsnorm(x_ref[0], g_ref[...]).astype(BF16)

    def seg(c0, n):
        return _dot(h_scr[...], w_ref[:, c0:c0 + n])

    f = seg(OFF_F, LANES) + bf_ref[...]
    lf = jnp.minimum(f, 0.0) - jnp.log1p(jnp.exp(-jnp.abs(f)))
    row = lax.broadcasted_iota(jnp.int32, (tm, tm), 0)
    col = lax.broadcasted_iota(jnp.int32, (tm, tm), 1)
    tri = jnp.where(col <= row, 1.0, 0.0).astype(BF16)
    hi, mid, lo = _split3(lf)
    c = _dot(tri, hi) + _dot(tri, mid) + _dot(tri, lo) + carry_scr[0:1, :]
    carry_scr[0:1, :] = c[tm - 1:tm, :]
    c3 = jnp.concatenate(_split3(c * LOG2E), axis=1)

    lane = lax.broadcasted_iota(jnp.int32, (tm, FOX_PAD_W), 1) & (LANES - 1)
    ones_q = jnp.where((lane >= AUG_K_C) & (lane < AUG_K_C + 3), 1.0, 0.0)
    ones_k = jnp.where((lane >= AUG_Q_C) & (lane < AUG_Q_C + 3), 1.0, 0.0)
    qa_ref[0] = (seg(OFF_QA, FOX_PAD_W) + _dot(c3, pq_ref[...]) + ones_q).T.astype(BF16)
    ka_ref[0] = (seg(OFF_KA, FOX_PAD_W) + _dot(c3, pk_ref[...]) + ones_k).astype(BF16)
    va_ref[0] = seg(OFF_VA, FOX_W).T.astype(BF16)

    rc, rs1, rs2 = rc_ref[...], rs1_ref[...], rs2_ref[...]
    reps = NSA_W // LANES
    qb_ref[0] = _rope(seg(OFF_QB, NSA_W), _lane_tile(rc, reps), _lane_tile(rs1, reps),
                      _lane_tile(rs2, reps)).T.astype(BF16)
    pos = t * tm + lax.broadcasted_iota(jnp.int32, (tm, LANES), 0)
    onehot = jnp.where((pos >> 6) == lax.broadcasted_iota(jnp.int32, (tm, LANES), 1), 1.0, 0.0)
    ks_ref[0] = jnp.concatenate([_rope(seg(OFF_KS, KV_W), rc, rs1, rs2), onehot], axis=1).astype(BF16)
    kw_ref[0] = _rope(seg(OFF_KW, KV_W), rc, rs1, rs2).astype(BF16)
    vs_ref[0] = seg(OFF_VS, KV_W).T.astype(BF16)
    vw_ref[0] = seg(OFF_VW, KV_W).T.astype(BF16)
    kvc_ref[0, 0] = seg(OFF_KC, KV_W)
    kvc_ref[1, 0] = seg(OFF_VC, KV_W)
    gb_ref[0] = seg(OFF_GB, LANES)
    gate_ref[0, :, 0:D_MODEL] = seg(OFF_GATE, D_MODEL)
    gate_ref[0, :, D_MODEL:2 * D_MODEL] = seg(OFF_GATE + D_MODEL, D_MODEL)


def _proj(layer, x, g, wp, bf, pq, pk, rc, rs1, rs2):
    B, T, _ = x.shape
    tm = PROJ_TM
    row = lambda w: pl.BlockSpec((1, tm, w), lambda b, t: (b, t, 0))
    col = lambda w: pl.BlockSpec((1, w, tm), lambda b, t: (b, 0, t))
    const = lambda shape: pl.BlockSpec(shape, lambda b, t: (0,) * len(shape))
    tab = pl.BlockSpec((tm, LANES), lambda b, t: (t, 0))
    out_shapes = [
        jax.ShapeDtypeStruct((B, FOX_PAD_W, T), BF16),
        jax.ShapeDtypeStruct((B, T, FOX_PAD_W), BF16),
        jax.ShapeDtypeStruct((B, FOX_W, T), BF16),
        jax.ShapeDtypeStruct((B, NSA_W, T), BF16),
        jax.ShapeDtypeStruct((B, T, KV_W + LANES), BF16),
        jax.ShapeDtypeStruct((B, T, KV_W), BF16),
        jax.ShapeDtypeStruct((B, KV_W, T), BF16),
        jax.ShapeDtypeStruct((B, KV_W, T), BF16),
        jax.ShapeDtypeStruct((2, B, T, KV_W), F32),
        jax.ShapeDtypeStruct((B, T, LANES), F32),
        jax.ShapeDtypeStruct((B, T, 2 * D_MODEL), F32),
    ]
    out_specs = [
        col(FOX_PAD_W), row(FOX_PAD_W), col(FOX_W), col(NSA_W),
        row(KV_W + LANES), row(KV_W), col(KV_W), col(KV_W),
        pl.BlockSpec((2, 1, tm, KV_W), lambda b, t: (0, b, t, 0)),
        row(LANES), row(2 * D_MODEL),
    ]
    return pl.pallas_call(
        functools.partial(_proj_kernel, tm=tm),
        grid=(B, T // tm),
        in_specs=[row(D_MODEL), _layer_spec(layer, (1, D_MODEL)), _layer_spec(layer, (D_MODEL, PROJ_W)),
                  _layer_spec(layer, (1, LANES)),
                  const((3 * LANES, FOX_PAD_W)), const((3 * LANES, FOX_PAD_W)), tab, tab, tab],
        out_specs=out_specs,
        out_shape=out_shapes,
        scratch_shapes=[pltpu.VMEM((tm, D_MODEL), BF16), pltpu.VMEM((8, LANES), F32)],
        compiler_params=_cparams(("arbitrary", "arbitrary")),
        name="proj",
    )(x, g, wp, bf, pq, pk, rc, rs1, rs2)


def _with_ones_rows(vt):
    first = lax.broadcasted_iota(jnp.int32, (DEN_ROWS, vt.shape[1]), 0) == 0
    return jnp.concatenate([vt, jnp.where(first, 1.0, 0.0).astype(vt.dtype)], axis=0)


def _causal_mask_t(st, i, j, tq, tk):
    kpos = j * tk + lax.broadcasted_iota(jnp.int32, (tk, tq), 0)
    qpos = i * tq + lax.broadcasted_iota(jnp.int32, (tk, tq), 1)
    return jnp.where(kpos <= qpos, st, NEG_BIG)


def _causal_flash_t(i, n_heads, tq, tk, score_fn, vt_fn, scr):
    s0_scr, s1_scr, m_scr, acc_scr = scr
    n_full = (i * tq) // tk

    def scores(j, s_scr):
        start = pl.multiple_of(j * tk, tk)
        for h in range(n_heads):
            s_scr[h] = score_fn(h, start)

    def softmax_pv(j, s_scr, masked):
        start = pl.multiple_of(j * tk, tk)
        for h in range(n_heads):
            st = s_scr[h]
            if masked:
                st = _causal_mask_t(st, i, j, tq, tk)
            m = m_scr[h]
            m_new = jnp.maximum(m, jnp.max(st, axis=0, keepdims=True))
            p = jnp.exp2(st - m_new).astype(BF16)
            acc_scr[h] = jnp.exp2(m - m_new) * acc_scr[h] + _dot(_with_ones_rows(vt_fn(h, start)), p)
            m_scr[h] = m_new

    m_scr[...] = jnp.full_like(m_scr, NEG_BIG)
    acc_scr[...] = jnp.zeros_like(acc_scr)

    odd = n_full & 1

    @pl.when(odd == 1)
    def _():
        scores(0, s1_scr)
        softmax_pv(0, s1_scr, False)

    scores(odd, s0_scr)

    def body(jp, carry):
        t0 = odd + 2 * jp
        scores(t0 + 1, s1_scr)
        softmax_pv(t0, s0_scr, False)
        scores(t0 + 2, s0_scr)
        softmax_pv(t0 + 1, s1_scr, False)
        return carry

    lax.fori_loop(0, n_full // 2, body, 0)
    softmax_pv(n_full, s0_scr, True)
    return jnp.concatenate([acc_scr[h, 0:HEAD_DIM, :] / acc_scr[h, HEAD_DIM:HEAD_DIM + 1, :]
                            for h in range(n_heads)], axis=0)


def _flash_scratch(n_heads, tq, tk):
    return [pltpu.VMEM((n_heads, tk, tq), F32), pltpu.VMEM((n_heads, tk, tq), F32),
            pltpu.VMEM((n_heads, 1, tq), F32), pltpu.VMEM((n_heads, HEAD_DIM + DEN_ROWS, tq), F32)]


def _fox_kernel(qt_ref, k_ref, vt_ref, o_ref, *scr, tq, tk, heads):
    def score_fn(h, start):
        return _dot(k_ref[0, pl.ds(start, tk), h * LANES:(h + 1) * LANES], qt_ref[0, h * LANES:(h + 1) * LANES, :])

    def vt_fn(h, start):
        return vt_ref[0, h * HEAD_DIM:(h + 1) * HEAD_DIM, pl.ds(start, tk)]

    o_ref[0] = _causal_flash_t(pl.program_id(2), heads, tq, tk, score_fn, vt_fn, scr).T.astype(BF16)


def _fox(qat, ka, vat):
    B, T, _ = ka.shape
    tq, tk = min(FOX_TQ, T), min(FOX_TK, T)
    heads = FOX_HEADS_PER_STEP
    return pl.pallas_call(
        functools.partial(_fox_kernel, tq=tq, tk=tk, heads=heads),
        grid=(B, FOX_HEADS // heads, T // tq),
        in_specs=[pl.BlockSpec((1, heads * LANES, tq), lambda b, h, i: (b, h, i)),
                  pl.BlockSpec((1, T, heads * LANES), lambda b, h, i: (b, 0, h)),
                  pl.BlockSpec((1, heads * HEAD_DIM, T), lambda b, h, i: (b, h, 0))],
        out_specs=pl.BlockSpec((1, tq, heads * HEAD_DIM), lambda b, h, i: (b, i, h)),
        out_shape=jax.ShapeDtypeStruct((B, T, FOX_W), BF16),
        scratch_shapes=_flash_scratch(heads, tq, tk),
        compiler_params=_cparams(("arbitrary", "arbitrary", "arbitrary")),
        name="fox",
    )(qat, ka, vat)


def _cmp_kernel(r_ref, w1_ref, pos_ref, b1_ref, w2_ref, b2_ref, rc_ref, rs1_ref, rs2_ref, o_ref, ot_ref):
    nr = o_ref.shape[0]
    half = CMP_BLOCK // 2
    a = bm = None
    for l in range(half):
        x = r_ref[pl.ds(l, nr, stride=CMP_STRIDE), :]
        ta = _dot((x + pos_ref[l:l + 1, :]).astype(BF16), w1_ref[0, l * KV_W:(l + 1) * KV_W, :])
        tb = _dot((x + pos_ref[half + l:half + l + 1, :]).astype(BF16), w1_ref[1, l * KV_W:(l + 1) * KV_W, :])
        a = ta if a is None else a + ta
        bm = tb if bm is None else bm + tb
    hdn = a + pltpu.roll(bm, nr - 1, 0) + b1_ref[...]
    act = 0.5 * hdn * (1.0 + jnp.tanh(0.7978845608028654 * (hdn + 0.044715 * (hdn * hdn * hdn))))
    y = _rope(_dot(act.astype(BF16), w2_ref[...]) + b2_ref[...], rc_ref[...], rs1_ref[...], rs2_ref[...])
    o_ref[...] = y.astype(BF16)
    ot_ref[...] = y.T.astype(BF16)


def _cmp(layer, kvc, w1, pos, b1, w2, b2, rc, rs1, rs2):
    _, B, T, _ = kvc.shape
    nr = T // CMP_STRIDE
    rw = CMP_STRIDE * KV_W
    per_kv = lambda shape: pl.BlockSpec((None, None) + shape, lambda s, b: (layer, s) + (0,) * len(shape))
    tab = pl.BlockSpec((None, nr, LANES), lambda s, b: (s, 0, 0))
    return pl.pallas_call(
        _cmp_kernel,
        grid=(2, B),
        in_specs=[pl.BlockSpec((None, None, T, KV_W), lambda s, b: (s, b, 0, 0)),
                  per_kv((2, rw, 2 * CMP_HIDDEN)), per_kv((CMP_BLOCK, KV_W)), per_kv((1, 2 * CMP_HIDDEN)),
                  per_kv((2 * CMP_HIDDEN, KV_W)), per_kv((1, KV_W)), tab, tab, tab],
        out_specs=[pl.BlockSpec((None, None, nr, KV_W), lambda s, b: (s, b, 0, 0)),
                   pl.BlockSpec((None, None, KV_W, nr), lambda s, b: (s, b, 0, 0))],
        out_shape=[jax.ShapeDtypeStruct((2, B, nr, KV_W), BF16), jax.ShapeDtypeStruct((2, B, KV_W, nr), BF16)],
        compiler_params=_cparams(("arbitrary", "arbitrary")),
        name="cmp",
    )(kvc, w1, pos, b1, w2, b2, rc, rs1, rs2)


def _group_query_weights(qt, g):
    zero = jnp.zeros((HEAD_DIM, qt.shape[1]), qt.dtype)
    out = []
    for r in range(NSA_GROUP):
        q = qt[r * HEAD_DIM:(r + 1) * HEAD_DIM, :]
        out.append(jnp.concatenate([jnp.where(g == 0, q, zero), jnp.where(g == 1, q, zero)], axis=0))
    return out


def _cmpsel_kernel(qt_ref, kc_ref, vct_ref, mapt_ref, oc_ref, mt_ref, *, tq, n_sel, top):
    g = pl.program_id(1)
    i = pl.program_id(2)
    nr = kc_ref.shape[2]
    kc = kc_ref[0, 0]
    vct = vct_ref[0, 0]
    cend = lax.broadcasted_iota(jnp.int32, (nr, tq), 0) * CMP_STRIDE + (CMP_BLOCK - 1)
    qpos = i * tq + lax.broadcasted_iota(jnp.int32, (nr, tq), 1)
    valid = cend <= qpos
    sts = [_dot(kc, w) for w in _group_query_weights(qt_ref[0], g)]
    imp = jnp.zeros((LANES, tq), F32)
    outs = []
    for st in sts:
        st = jnp.where(valid, st, NEG_BIG)
        e = jnp.exp2(st - jnp.max(st, axis=0, keepdims=True))
        p = jnp.where(valid, e / jnp.sum(e, axis=0, keepdims=True), 0.0).astype(BF16)
        outs.append(_dot(vct, p))
        imp = imp + _dot(mapt_ref[...], p)
    oc_ref[0] = jnp.concatenate(outs, axis=0).T

    sid = lax.broadcasted_iota(jnp.int32, (LANES, tq), 0)
    qp = i * tq + lax.broadcasted_iota(jnp.int32, (LANES, tq), 1)
    sidf = sid.astype(F32)
    score = jnp.where(sid * SEL_BLOCK > qp, -1.0,
                      jnp.where((sid == (qp >> 6)) | (sid == 0), FORCE_SCORE, imp))
    score = jnp.where(sid < n_sel, score, -3.0)
    maskt = jnp.full((LANES, tq), NEG_BIG, F32)
    for _ in range(top):
        mx = jnp.max(score, axis=0, keepdims=True)
        idx = jnp.min(jnp.where(score == mx, sidf, float(LANES)), axis=0, keepdims=True)
        pick = sidf == idx
        maskt = jnp.where(pick, 0.0, maskt)
        score = jnp.where(pick, -2.0, score)
    mt_ref[0, 0] = maskt.astype(BF16)


def _cmpsel(qbt, kc, vct, selmap_t):
    B, _, T = qbt.shape
    nr = kc.shape[2]
    tq = NSA_TQ
    n_sel = T // SEL_BLOCK
    return pl.pallas_call(
        functools.partial(_cmpsel_kernel, tq=tq, n_sel=n_sel, top=min(SEL_TOPK, n_sel)),
        grid=(B, NSA_KV_HEADS, T // tq),
        in_specs=[pl.BlockSpec((1, GROUP_W, tq), lambda b, g, i: (b, g, i)),
                  pl.BlockSpec((1, 1, nr, KV_W), lambda b, g, i: (0, b, 0, 0)),
                  pl.BlockSpec((1, 1, HEAD_DIM, nr), lambda b, g, i: (1, b, g, 0)),
                  pl.BlockSpec((LANES, nr), lambda b, g, i: (0, 0))],
        out_specs=[pl.BlockSpec((1, tq, GROUP_W), lambda b, g, i: (b, i, g)),
                   pl.BlockSpec((1, 1, LANES, tq), lambda b, g, i: (b, g, 0, i))],
        out_shape=[jax.ShapeDtypeStruct((B, T, NSA_W), F32),
                   jax.ShapeDtypeStruct((B, NSA_KV_HEADS, LANES, T), BF16)],
        compiler_params=_cparams(("arbitrary", "arbitrary", "arbitrary")),
        name="cmpsel",
    )(qbt, kc, vct, selmap_t)


def _sel_kernel(qt_ref, mt_ref, k_ref, vt_ref, o_ref, w_scr, *scr, tq, tk):
    g = pl.program_id(1)
    for r, w in enumerate(_group_query_weights(qt_ref[0], g)):
        w_scr[r] = jnp.concatenate([w, mt_ref[0, 0]], axis=0)

    def score_fn(r, start):
        return _dot(k_ref[0, pl.ds(start, tk), :], w_scr[r])

    def vt_fn(r, start):
        return vt_ref[0, :, pl.ds(start, tk)]

    o_ref[0] = _causal_flash_t(pl.program_id(2), NSA_GROUP, tq, tk, score_fn, vt_fn, scr).T


def _sel(qbt, maskt, ks, vst):
    B, _, T = qbt.shape
    tq, tk = min(SEL_TQ, T), min(SEL_TK, T)
    kw = KV_W + LANES
    return pl.pallas_call(
        functools.partial(_sel_kernel, tq=tq, tk=tk),
        grid=(B, NSA_KV_HEADS, T // tq),
        in_specs=[pl.BlockSpec((1, GROUP_W, tq), lambda b, g, i: (b, g, i)),
                  pl.BlockSpec((1, 1, LANES, tq), lambda b, g, i: (b, g, 0, i)),
                  pl.BlockSpec((1, T, kw), lambda b, g, i: (b, 0, 0)),
                  pl.BlockSpec((1, HEAD_DIM, T), lambda b, g, i: (b, g, 0))],
        out_specs=pl.BlockSpec((1, tq, GROUP_W), lambda b, g, i: (b, i, g)),
        out_shape=jax.ShapeDtypeStruct((B, T, NSA_W), F32),
        scratch_shapes=[pltpu.VMEM((NSA_GROUP, kw, tq), BF16)] + _flash_scratch(NSA_GROUP, tq, tk),
        compiler_params=_cparams(("arbitrary", "arbitrary", "arbitrary")),
        name="sel",
    )(qbt, maskt, ks, vst)


def _win_kernel(qt_ref, k_ref, vt_ref, o_ref, *, tq):
    g = pl.program_id(1)
    i = pl.program_id(2)
    wlen = WINDOW + tq
    start = pl.multiple_of(jnp.maximum(i * tq - WINDOW, 0), tq)
    k = k_ref[0, pl.ds(start, wlen), :]
    vt = vt_ref[0, :, pl.ds(start, wlen)]
    kpos = start + lax.broadcasted_iota(jnp.int32, (wlen, tq), 0)
    qpos = i * tq + lax.broadcasted_iota(jnp.int32, (wlen, tq), 1)
    diff = qpos - kpos
    valid = (diff >= 0) & (diff < WINDOW)
    sts = [_dot(k, w) for w in _group_query_weights(qt_ref[0], g)]
    outs = []
    for st in sts:
        st = jnp.where(valid, st, NEG_BIG)
        e = jnp.exp2(st - jnp.max(st, axis=0, keepdims=True))
        outs.append(_dot(vt, e.astype(BF16)) / jnp.sum(e, axis=0, keepdims=True))
    o_ref[0] = jnp.concatenate(outs, axis=0).T


def _win(qbt, kw, vwt):
    B, _, T = qbt.shape
    tq = NSA_TQ
    return pl.pallas_call(
        functools.partial(_win_kernel, tq=tq),
        grid=(B, NSA_KV_HEADS, T // tq),
        in_specs=[pl.BlockSpec((1, GROUP_W, tq), lambda b, g, i: (b, g, i)),
                  pl.BlockSpec((1, T, KV_W), lambda b, g, i: (b, 0, 0)),
                  pl.BlockSpec((1, HEAD_DIM, T), lambda b, g, i: (b, g, 0))],
        out_specs=pl.BlockSpec((1, tq, GROUP_W), lambda b, g, i: (b, i, g)),
        out_shape=jax.ShapeDtypeStruct((B, T, NSA_W), F32),
        compiler_params=_cparams(("arbitrary", "arbitrary", "arbitrary")),
        name="win",
    )(qbt, kw, vwt)


def _merge_kernel(x_ref, oa_ref, oc_ref, os_ref, ow_ref, gb_ref, gate_ref, wof_ref, won_ref, wout_ref, e_ref, o_ref):
    ya = _dot(oa_ref[...], wof_ref[...])
    sg = jnp.concatenate(_split3(_sigmoid(gb_ref[...])), axis=1)
    on = (_dot(sg, e_ref[0]) * oc_ref[...] + _dot(sg, e_ref[1]) * os_ref[...]
          + _dot(sg, e_ref[2]) * ow_ref[...])
    yb = _dot(on.astype(BF16), won_ref[...])
    mixed = _sigmoid(gate_ref[:, 0:D_MODEL]) * ya + _sigmoid(gate_ref[:, D_MODEL:2 * D_MODEL]) * yb
    o_ref[...] = x_ref[...] + _dot(mixed.astype(BF16), wout_ref[...])


def _layer_spec(layer, shape):
    return pl.BlockSpec((None,) + shape, lambda *_: (layer,) + (0,) * len(shape))


def _merge(layer, x, oa, oc, os_, ow, gb, gate, wof, won, wout, e):
    n = x.shape[0]
    tm = MERGE_TM
    row = lambda w: pl.BlockSpec((tm, w), lambda t: (t, 0))
    const = lambda shape: pl.BlockSpec(shape, lambda t: (0,) * len(shape))
    return pl.pallas_call(
        _merge_kernel,
        grid=(n // tm,),
        in_specs=[row(D_MODEL), row(FOX_W), row(NSA_W), row(NSA_W), row(NSA_W), row(LANES), row(2 * D_MODEL),
                  _layer_spec(layer, (FOX_W, D_MODEL)), _layer_spec(layer, (NSA_W, D_MODEL)),
                  _layer_spec(layer, (D_MODEL, D_MODEL)), const((3, 3 * LANES, NSA_W))],
        out_specs=row(D_MODEL),
        out_shape=jax.ShapeDtypeStruct((n, D_MODEL), F32),
        compiler_params=_cparams(("arbitrary",)),
        name="merge",
    )(x, oa, oc, os_, ow, gb, gate, wof, won, wout, e)


def _mlp_kernel(x_ref, g_ref, wu_ref, wd_ref, gf_ref, o_ref, h_scr, *, final_norm):
    x = x_ref[...]
    h_scr[...] = _rmsnorm(x, g_ref[...]).astype(BF16)
    acc = None
    for c0 in range(0, D_FF, MLP_FF_CHUNK):
        u = jnp.maximum(_dot(h_scr[...], wu_ref[:, c0:c0 + MLP_FF_CHUNK]), 0.0)
        part = _dot((u * u).astype(BF16), wd_ref[c0:c0 + MLP_FF_CHUNK, :])
        acc = part if acc is None else acc + part
    y = x + acc
    o_ref[...] = _rmsnorm(y, gf_ref[...]) if final_norm else y


def _mlp(layer, x, g, wu, wd, gf, final_norm):
    n = x.shape[0]
    tm = MLP_TM
    row = pl.BlockSpec((tm, D_MODEL), lambda t: (t, 0))
    const = lambda shape: pl.BlockSpec(shape, lambda t: (0,) * len(shape))
    return pl.pallas_call(
        functools.partial(_mlp_kernel, final_norm=final_norm),
        grid=(n // tm,),
        in_specs=[row, _layer_spec(layer, (1, D_MODEL)), _layer_spec(layer, (D_MODEL, D_FF)),
                  _layer_spec(layer, (D_FF, D_MODEL)), const((1, D_MODEL))],
        out_specs=row,
        out_shape=jax.ShapeDtypeStruct((n, D_MODEL), F32),
        scratch_shapes=[pltpu.VMEM((tm, D_MODEL), BF16)],
        compiler_params=_cparams(("arbitrary",)),
        name="mlp",
    )(x, g, wu, wd, gf)


def _rope_tables(pos):
    inv = ROPE_THETA ** (-jnp.arange(ROPE_HALF, dtype=F32) / ROPE_HALF)
    ang = pos.astype(F32)[:, None] * inv[None, :]
    cos, sin = jnp.cos(ang), jnp.sin(ang)
    n = pos.shape[0]
    rest = HEAD_DIM - ROPE_DIM
    rc = jnp.concatenate([cos, cos, jnp.ones((n, rest), F32)], axis=1)
    rs1 = jnp.concatenate([-sin, jnp.zeros((n, HEAD_DIM - ROPE_HALF), F32)], axis=1)
    rs2 = jnp.concatenate([jnp.zeros((n, ROPE_HALF), F32), sin, jnp.zeros((n, rest), F32)], axis=1)
    two = lambda t: jnp.concatenate([t, t], axis=1)
    return two(rc), two(rs1), two(rs2)


def _pad_heads(w, n_heads):
    lead = w.shape[:-1]
    w = w.reshape(lead + (n_heads, HEAD_DIM))
    return jnp.concatenate([w, jnp.zeros_like(w)], axis=-1).reshape(lead + (n_heads * LANES,))


def _pad_cols(w, width):
    return jnp.concatenate([w, jnp.zeros(w.shape[:-1] + (width - w.shape[-1],), w.dtype)], axis=-1)


def _proj_weight(w_in):
    o = np.cumsum([0, FOX_W, FOX_W, FOX_W, FOX_HEADS, NSA_W, 6 * KV_W, 3 * NSA_HEADS, D_MODEL, D_MODEL])
    cols = lambda j: w_in[..., o[j]:o[j + 1]]
    qscale = SCALE * LOG2E
    parts = [
        _pad_heads(cols(0) * qscale, FOX_HEADS), _pad_heads(cols(1), FOX_HEADS), cols(2),
        _pad_cols(cols(3), LANES),
        cols(4) * qscale,
        cols(5),
        _pad_cols(cols(6), LANES),
        cols(7), cols(8),
    ]
    return jnp.concatenate(parts, axis=-1).astype(BF16)


def _aug_placement():
    pq = np.zeros((3 * LANES, FOX_PAD_W), np.float32)
    pk = np.zeros((3 * LANES, FOX_PAD_W), np.float32)
    for h in range(FOX_HEADS):
        for piece in range(3):
            pq[piece * LANES + h, h * LANES + AUG_Q_C + piece] = 1.0
            pk[piece * LANES + h, h * LANES + AUG_K_C + piece] = -1.0
    return jnp.asarray(pq, BF16), jnp.asarray(pk, BF16)


def _gate_placement():
    e = np.zeros((3, 3 * LANES, NSA_W), np.float32)
    for j in range(3):
        for h in range(NSA_HEADS):
            for piece in range(3):
                e[j, piece * LANES + 3 * h + j, h * HEAD_DIM:(h + 1) * HEAD_DIM] = 1.0
    return jnp.asarray(e, BF16)


def _sel_map_t(T, nr):
    nc = (T - CMP_BLOCK) // CMP_STRIDE + 1
    n_sel = T // SEL_BLOCK
    cs = np.arange(nc) * CMP_STRIDE
    ce = cs + CMP_BLOCK
    ss = np.arange(n_sel) * SEL_BLOCK
    se = ss + SEL_BLOCK
    ov = np.clip(np.minimum(ce[None, :], se[:, None]) - np.maximum(cs[None, :], ss[:, None]), 0, None)
    m = np.zeros((LANES, nr), np.float32)
    m[:n_sel, :nc] = ov / CMP_BLOCK
    return jnp.asarray(m, BF16)


def _cmp_weights(cmp_pos, cmp_w1, cmp_b1, cmp_w2, cmp_b2):
    d, half, g2 = cmp_w1.shape[0], CMP_BLOCK // 2, NSA_KV_HEADS
    eye = jnp.eye(g2, dtype=F32)
    w1 = cmp_w1.reshape(d, 2, 2, half, 1, HEAD_DIM, 1, CMP_HIDDEN) * eye[:, None, :, None]
    w1 = w1.reshape(d, 2, 2, half * KV_W, g2 * CMP_HIDDEN).astype(BF16)
    pos = jnp.concatenate([cmp_pos] * g2, axis=-1)
    b1 = jnp.concatenate([cmp_b1] * g2, axis=-1)[:, :, None, :]
    w2 = (cmp_w2.reshape(d, 2, 1, CMP_HIDDEN, 1, HEAD_DIM) * eye[:, None, :, None])
    w2 = w2.reshape(d, 2, g2 * CMP_HIDDEN, KV_W).astype(BF16)
    b2 = jnp.concatenate([cmp_b2] * g2, axis=-1)[:, :, None, :]
    return w1, pos, b1, w2, b2


def kernel(x, norm_mix, w_in, b_forget, cmp_pos, cmp_w1, cmp_b1, cmp_w2, cmp_b2, w_o_fox, w_o_nsa, w_out,
           norm_mlp, w_up, w_down, norm_final):
    B, T, _ = x.shape
    depth = w_in.shape[0]
    nr = T // CMP_STRIDE
    n = B * T
    flat = lambda a: a.reshape(n, a.shape[-1])

    pq, pk = _aug_placement()
    gate_e = _gate_placement()
    selmap_t = _sel_map_t(T, nr)
    rope = _rope_tables(jnp.arange(T))
    crope = [jnp.stack([t, jnp.full_like(t, fill)]) for t, fill in
             zip(_rope_tables(jnp.arange(nr) * CMP_STRIDE + CMP_BLOCK - 1), (1.0, 0.0, 0.0))]
    wp = _proj_weight(w_in)
    bf = _pad_cols(b_forget, LANES)[:, None, :]
    cmp_w = _cmp_weights(cmp_pos, cmp_w1, cmp_b1, cmp_w2, cmp_b2)
    wof, won, wout = w_o_fox.astype(BF16), w_o_nsa.astype(BF16), w_out.astype(BF16)
    wu, wd = w_up.astype(BF16), w_down.astype(BF16)
    g_mix, g_mlp = norm_mix[:, None, :], norm_mlp[:, None, :]

    for l in range(depth):
        qat, ka, vat, qbt, ks, kw, vst, vwt, kvc, gb, gate = _proj(l, x, g_mix, wp, bf, pq, pk, *rope)
        oa = _fox(qat, ka, vat)
        cmp_rows, cmp_cols = _cmp(l, kvc, *cmp_w, *crope)
        oc, maskt = _cmpsel(qbt, cmp_rows, cmp_cols, selmap_t)
        os_ = _sel(qbt, maskt, ks, vst)
        ow = _win(qbt, kw, vwt)
        x1 = _merge(l, flat(x), flat(oa), flat(oc), flat(os_), flat(ow), flat(gb), flat(gate),
                    wof, won, wout, gate_e)
        x = _mlp(l, x1, g_mlp, wu, wd, norm_final[None, :], l == depth - 1).reshape(B, T, D_MODEL)
    return x
```

```python
import functools

import numpy as np
import jax
import jax.numpy as jnp
from jax import lax
from jax.experimental import pallas as pl
from jax.experimental.pallas import tpu as pltpu

F32 = jnp.float32
BF16 = jnp.bfloat16

D_MODEL = 1024
HEAD_DIM = 64
FOX_HEADS = 8
NSA_HEADS = 8
NSA_KV_HEADS = 2
NSA_GROUP = NSA_HEADS // NSA_KV_HEADS
CMP_BLOCK = 32
CMP_STRIDE = 16
CMP_HIDDEN = 4 * HEAD_DIM
SEL_BLOCK = 64
SEL_TOPK = 16
WINDOW = 512
ROPE_THETA = 500000.0
ROPE_DIM = HEAD_DIM // 4
ROPE_HALF = ROPE_DIM // 2
D_FF = 4 * D_MODEL
RMS_EPS = 1e-6
FORCE_SCORE = 1e6
NEG_BIG = -1e30
SCALE = HEAD_DIM ** -0.5
LOG2E = 1.4426950408889634

FOX_W = FOX_HEADS * HEAD_DIM
NSA_W = NSA_HEADS * HEAD_DIM
KV_W = NSA_KV_HEADS * HEAD_DIM

LANES = 128
DEN_ROWS = 16
VMEM_LIMIT_BYTES = 56 * 1024 * 1024

FOX_PAD_W = FOX_HEADS * LANES
GROUP_W = NSA_GROUP * HEAD_DIM
OFF_QA = 0
OFF_KA = OFF_QA + FOX_PAD_W
OFF_VA = OFF_KA + FOX_PAD_W
OFF_F = OFF_VA + FOX_W
OFF_QB = OFF_F + LANES
OFF_KC = OFF_QB + NSA_W
OFF_VC = OFF_KC + KV_W
OFF_KS = OFF_VC + KV_W
OFF_VS = OFF_KS + KV_W
OFF_KW = OFF_VS + KV_W
OFF_VW = OFF_KW + KV_W
OFF_GB = OFF_VW + KV_W
OFF_GATE = OFF_GB + LANES
PROJ_W = OFF_GATE + 2 * D_MODEL

AUG_Q_C = HEAD_DIM
AUG_K_C = HEAD_DIM + 3

PROJ_TM = 256
FOX_TQ = 512
FOX_TK = 512
FOX_HEADS_PER_STEP = 4
NSA_TQ = 256
SEL_TQ = 512
SEL_TK = 512
MERGE_TM = 256
MLP_TM = 512
MLP_FF_CHUNK = 512


def _cparams(sem):
    return pltpu.CompilerParams(dimension_semantics=sem, vmem_limit_bytes=VMEM_LIMIT_BYTES)


def _dot(a, b):
    return jnp.dot(a, b, preferred_element_type=F32)


def _split3(x):
    hi = x.astype(BF16)
    r = x - hi.astype(F32)
    mid = r.astype(BF16)
    lo = (r - mid.astype(F32)).astype(BF16)
    return hi, mid, lo


def _lane_tile(x, reps):
    return x if reps == 1 else jnp.concatenate([x] * reps, axis=1)


def _rope(y, rc, rs1, rs2):
    n = y.shape[1]
    return y * rc + pltpu.roll(y, n - ROPE_HALF, 1) * rs1 + pltpu.roll(y, ROPE_HALF, 1) * rs2


def _rmsnorm(x, g):
    return x * lax.rsqrt(jnp.mean(x * x, axis=-1, keepdims=True) + RMS_EPS) * g


def _sigmoid(x):
    return 1.0 / (1.0 + jnp.exp(-x))


def _proj_kernel(x_ref, g_ref, w_ref, bf_ref, pq_ref, pk_ref, rc_ref, rs1_ref, rs2_ref,
                 qa_ref, ka_ref, va_ref, qb_ref, ks_ref, kw_ref, vs_ref, vw_ref, kvc_ref, gb_ref,
                 h_scr, carry_scr, *, tm):
    t = pl.program_id(1)

    @pl.when(t == 0)
    def _():
        carry_scr[...] = jnp.zeros_like(carry_scr)

    h_scr[...] = _rmsnorm(x_ref[0], g_ref[...]).astype(BF16)

    def seg(c0, n):
        return _dot(h_scr[...], w_ref[:, c0:c0 + n])

    f = seg(OFF_F, LANES) + bf_ref[...]
    lf = jnp.minimum(f, 0.0) - jnp.log1p(jnp.exp(-jnp.abs(f)))
    row = lax.broadcasted_iota(jnp.int32, (tm, tm), 0)
    col = lax.broadcasted_iota(jnp.int32, (tm, tm), 1)
    tri = jnp.where(col <= row, 1.0, 0.0).astype(BF16)
    hi, mid, lo = _split3(lf)
    c = _dot(tri, hi) + _dot(tri, mid) + _dot(tri, lo) + carry_scr[0:1, :]
    carry_scr[0:1, :] = c[tm - 1:tm, :]
    c3 = jnp.concatenate(_split3(c * LOG2E), axis=1)

    lane = lax.broadcasted_iota(jnp.int32, (tm, FOX_PAD_W), 1) & (LANES - 1)
    ones_q = jnp.where((lane >= AUG_K_C) & (lane < AUG_K_C + 3), 1.0, 0.0)
    ones_k = jnp.where((lane >= AUG_Q_C) & (lane < AUG_Q_C + 3), 1.0, 0.0)
    qa_ref[0] = (seg(OFF_QA, FOX_PAD_W) + _dot(c3, pq_ref[...]) + ones_q).T.astype(BF16)
    ka_ref[0] = (seg(OFF_KA, FOX_PAD_W) + _dot(c3, pk_ref[...]) + ones_k).astype(BF16)
    va_ref[0] = seg(OFF_VA, FOX_W).T.astype(BF16)

    rc, rs1, rs2 = rc_ref[...], rs1_ref[...], rs2_ref[...]
    reps = NSA_W // LANES
    qb_ref[0] = _rope(seg(OFF_QB, NSA_W), _lane_tile(rc, reps), _lane_tile(rs1, reps),
                      _lane_tile(rs2, reps)).T.astype(BF16)
    pos = t * tm + lax.broadcasted_iota(jnp.int32, (tm, LANES), 0)
    onehot = jnp.where((pos >> 6) == lax.broadcasted_iota(jnp.int32, (tm, LANES), 1), 1.0, 0.0)
    ks_ref[0] = jnp.concatenate([_rope(seg(OFF_KS, KV_W), rc, rs1, rs2), onehot], axis=1).astype(BF16)
    kw_ref[0] = _rope(seg(OFF_KW, KV_W), rc, rs1, rs2).astype(BF16)
    vs_ref[0] = seg(OFF_VS, KV_W).T.astype(BF16)
    vw_ref[0] = seg(OFF_VW, KV_W).T.astype(BF16)
    kvc_ref[0, 0] = seg(OFF_KC, KV_W)
    kvc_ref[1, 0] = seg(OFF_VC, KV_W)
    gb_ref[0] = seg(OFF_GB, LANES)


def _proj(layer, x, g, wp, bf, pq, pk, rc, rs1, rs2):
    B, T, _ = x.shape
    tm = PROJ_TM
    row = lambda w: pl.BlockSpec((1, tm, w), lambda b, t: (b, t, 0))
    col = lambda w: pl.BlockSpec((1, w, tm), lambda b, t: (b, 0, t))
    const = lambda shape: pl.BlockSpec(shape, lambda b, t: (0,) * len(shape))
    tab = pl.BlockSpec((tm, LANES), lambda b, t: (t, 0))
    out_shapes = [
        jax.ShapeDtypeStruct((B, FOX_PAD_W, T), BF16),
        jax.ShapeDtypeStruct((B, T, FOX_PAD_W), BF16),
        jax.ShapeDtypeStruct((B, FOX_W, T), BF16),
        jax.ShapeDtypeStruct((B, NSA_W, T), BF16),
        jax.ShapeDtypeStruct((B, T, KV_W + LANES), BF16),
        jax.ShapeDtypeStruct((B, T, KV_W), BF16),
        jax.ShapeDtypeStruct((B, KV_W, T), BF16),
        jax.ShapeDtypeStruct((B, KV_W, T), BF16),
        jax.ShapeDtypeStruct((2, B, T, KV_W), F32),
        jax.ShapeDtypeStruct((B, T, LANES), F32),
    ]
    out_specs = [
        col(FOX_PAD_W), row(FOX_PAD_W), col(FOX_W), col(NSA_W),
        row(KV_W + LANES), row(KV_W), col(KV_W), col(KV_W),
        pl.BlockSpec((2, 1, tm, KV_W), lambda b, t: (0, b, t, 0)),
        row(LANES),
    ]
    return pl.pallas_call(
        functools.partial(_proj_kernel, tm=tm),
        grid=(B, T // tm),
        in_specs=[row(D_MODEL), _layer_spec(layer, (1, D_MODEL)), _layer_spec(layer, (D_MODEL, OFF_GATE)),
                  _layer_spec(layer, (1, LANES)),
                  const((3 * LANES, FOX_PAD_W)), const((3 * LANES, FOX_PAD_W)), tab, tab, tab],
        out_specs=out_specs,
        out_shape=out_shapes,
        scratch_shapes=[pltpu.VMEM((tm, D_MODEL), BF16), pltpu.VMEM((8, LANES), F32)],
        compiler_params=_cparams(("arbitrary", "arbitrary")),
        name="proj",
    )(x, g, wp, bf, pq, pk, rc, rs1, rs2)


def _with_ones_rows(vt):
    first = lax.broadcasted_iota(jnp.int32, (DEN_ROWS, vt.shape[1]), 0) == 0
    return jnp.concatenate([vt, jnp.where(first, 1.0, 0.0).astype(vt.dtype)], axis=0)


def _causal_mask_t(st, i, j, tq, tk):
    kpos = j * tk + lax.broadcasted_iota(jnp.int32, (tk, tq), 0)
    qpos = i * tq + lax.broadcasted_iota(jnp.int32, (tk, tq), 1)
    return jnp.where(kpos <= qpos, st, NEG_BIG)


def _causal_flash_t(i, n_heads, tq, tk, score_fn, vt_fn, scr, tail_fn=None):
    s0_scr, s1_scr, m_scr, acc_scr = scr
    n_full = (i * tq) // tk

    def scores(j, s_scr):
        start = pl.multiple_of(j * tk, tk)
        for h in range(n_heads):
            s_scr[h] = score_fn(h, start)

    def softmax_pv(j, s_scr, masked):
        start = pl.multiple_of(j * tk, tk)
        for h in range(n_heads):
            st = s_scr[h]
            if masked:
                st = _causal_mask_t(st, i, j, tq, tk)
            m = m_scr[h]
            m_new = jnp.maximum(m, jnp.max(st, axis=0, keepdims=True))
            p = jnp.exp2(st - m_new).astype(BF16)
            acc_scr[h] = jnp.exp2(m - m_new) * acc_scr[h] + _dot(_with_ones_rows(vt_fn(h, start)), p)
            m_scr[h] = m_new

    m_scr[...] = jnp.full_like(m_scr, NEG_BIG)
    acc_scr[...] = jnp.zeros_like(acc_scr)

    odd = n_full & 1

    @pl.when(odd == 1)
    def _():
        scores(0, s1_scr)
        scores(1, s0_scr)
        softmax_pv(0, s1_scr, False)

    @pl.when(odd == 0)
    def _():
        scores(0, s0_scr)

    def body(jp, carry):
        t0 = odd + 2 * jp
        scores(t0 + 1, s1_scr)
        softmax_pv(t0, s0_scr, False)
        scores(t0 + 2, s0_scr)
        softmax_pv(t0 + 1, s1_scr, False)
        return carry

    lax.fori_loop(0, n_full // 2, body, 0)
    tail = tail_fn() if tail_fn is not None else None
    softmax_pv(n_full, s0_scr, True)
    out = jnp.concatenate([acc_scr[h, 0:HEAD_DIM, :] / acc_scr[h, HEAD_DIM:HEAD_DIM + 1, :]
                           for h in range(n_heads)], axis=0)
    return out, tail


def _flash_scratch(n_heads, tq, tk):
    return [pltpu.VMEM((n_heads, tk, tq), F32), pltpu.VMEM((n_heads, tk, tq), F32),
            pltpu.VMEM((n_heads, 1, tq), F32), pltpu.VMEM((n_heads, HEAD_DIM + DEN_ROWS, tq), F32)]


def _fox_kernel(qt_ref, k_ref, vt_ref, o_ref, *scr, tq, tk, heads):
    def score_fn(h, start):
        return _dot(k_ref[0, pl.ds(start, tk), h * LANES:(h + 1) * LANES], qt_ref[0, h * LANES:(h + 1) * LANES, :])

    def vt_fn(h, start):
        return vt_ref[0, h * HEAD_DIM:(h + 1) * HEAD_DIM, pl.ds(start, tk)]

    ot, _ = _causal_flash_t(pl.program_id(2), heads, tq, tk, score_fn, vt_fn, scr)
    o_ref[0] = ot.T.astype(BF16)


def _fox(qat, ka, vat):
    B, T, _ = ka.shape
    tq, tk = min(FOX_TQ, T), min(FOX_TK, T)
    heads = FOX_HEADS_PER_STEP
    return pl.pallas_call(
        functools.partial(_fox_kernel, tq=tq, tk=tk, heads=heads),
        grid=(B, FOX_HEADS // heads, T // tq),
        in_specs=[pl.BlockSpec((1, heads * LANES, tq), lambda b, h, i: (b, h, i)),
                  pl.BlockSpec((1, T, heads * LANES), lambda b, h, i: (b, 0, h)),
                  pl.BlockSpec((1, heads * HEAD_DIM, T), lambda b, h, i: (b, h, 0))],
        out_specs=pl.BlockSpec((1, tq, heads * HEAD_DIM), lambda b, h, i: (b, i, h)),
        out_shape=jax.ShapeDtypeStruct((B, T, FOX_W), BF16),
        scratch_shapes=_flash_scratch(heads, tq, tk),
        compiler_params=_cparams(("arbitrary", "arbitrary", "arbitrary")),
        name="fox",
    )(qat, ka, vat)


def _cmp_kernel(r_ref, w1_ref, pos_ref, b1_ref, w2_ref, b2_ref, rc_ref, rs1_ref, rs2_ref, o_ref, ot_ref):
    nr = o_ref.shape[0]
    half = CMP_BLOCK // 2
    a = bm = None
    for l in range(half):
        x = r_ref[pl.ds(l, nr, stride=CMP_STRIDE), :]
        ta = _dot((x + pos_ref[l:l + 1, :]).astype(BF16), w1_ref[0, l * KV_W:(l + 1) * KV_W, :])
        tb = _dot((x + pos_ref[half + l:half + l + 1, :]).astype(BF16), w1_ref[1, l * KV_W:(l + 1) * KV_W, :])
        a = ta if a is None else a + ta
        bm = tb if bm is None else bm + tb
    hdn = a + pltpu.roll(bm, nr - 1, 0) + b1_ref[...]
    act = 0.5 * hdn * (1.0 + jnp.tanh(0.7978845608028654 * (hdn + 0.044715 * (hdn * hdn * hdn))))
    y = _rope(_dot(act.astype(BF16), w2_ref[...]) + b2_ref[...], rc_ref[...], rs1_ref[...], rs2_ref[...])
    o_ref[...] = y.astype(BF16)
    ot_ref[...] = y.T.astype(BF16)


def _cmp(layer, kvc, w1, pos, b1, w2, b2, rc, rs1, rs2):
    _, B, T, _ = kvc.shape
    nr = T // CMP_STRIDE
    rw = CMP_STRIDE * KV_W
    per_kv = lambda shape: pl.BlockSpec((None, None) + shape, lambda s, b: (layer, s) + (0,) * len(shape))
    tab = pl.BlockSpec((None, nr, LANES), lambda s, b: (s, 0, 0))
    return pl.pallas_call(
        _cmp_kernel,
        grid=(2, B),
        in_specs=[pl.BlockSpec((None, None, T, KV_W), lambda s, b: (s, b, 0, 0)),
                  per_kv((2, rw, 2 * CMP_HIDDEN)), per_kv((CMP_BLOCK, KV_W)), per_kv((1, 2 * CMP_HIDDEN)),
                  per_kv((2 * CMP_HIDDEN, KV_W)), per_kv((1, KV_W)), tab, tab, tab],
        out_specs=[pl.BlockSpec((None, None, nr, KV_W), lambda s, b: (s, b, 0, 0)),
                   pl.BlockSpec((None, None, KV_W, nr), lambda s, b: (s, b, 0, 0))],
        out_shape=[jax.ShapeDtypeStruct((2, B, nr, KV_W), BF16), jax.ShapeDtypeStruct((2, B, KV_W, nr), BF16)],
        compiler_params=_cparams(("arbitrary", "arbitrary")),
        name="cmp",
    )(kvc, w1, pos, b1, w2, b2, rc, rs1, rs2)


def _group_query_weights(qt, g):
    zero = jnp.zeros((HEAD_DIM, qt.shape[1]), qt.dtype)
    out = []
    for r in range(NSA_GROUP):
        q = qt[r * HEAD_DIM:(r + 1) * HEAD_DIM, :]
        out.append(jnp.concatenate([jnp.where(g == 0, q, zero), jnp.where(g == 1, q, zero)], axis=0))
    return out


def _cmpsel_kernel(qt_ref, kc_ref, vct_ref, mapt_ref, oc_ref, mt_ref, *, tq, n_sel, top):
    g = pl.program_id(1)
    i = pl.program_id(2)
    nr = kc_ref.shape[2]
    kc = kc_ref[0, 0]
    vct = vct_ref[0, 0]
    cend = lax.broadcasted_iota(jnp.int32, (nr, tq), 0) * CMP_STRIDE + (CMP_BLOCK - 1)
    qpos = i * tq + lax.broadcasted_iota(jnp.int32, (nr, tq), 1)
    valid = cend <= qpos
    sts = [_dot(kc, w) for w in _group_query_weights(qt_ref[0], g)]
    vct_ones = _with_ones_rows(vct)
    imp = jnp.zeros((LANES, tq), F32)
    outs = []
    for st in sts:
        st = jnp.where(valid, st, NEG_BIG)
        e = jnp.where(valid, jnp.exp2(st - jnp.max(st, axis=0, keepdims=True)), 0.0).astype(BF16)
        oe = _dot(vct_ones, e)
        den = oe[HEAD_DIM:HEAD_DIM + 1, :]
        inv = jnp.where(den > 0.0, 1.0 / jnp.where(den > 0.0, den, 1.0), 0.0)
        outs.append(oe[0:HEAD_DIM, :] * inv)
        imp = imp + _dot(mapt_ref[...], e) * inv
    oc_ref[0] = jnp.concatenate(outs, axis=0).T

    sid = lax.broadcasted_iota(jnp.int32, (LANES, tq), 0)
    qp = i * tq + lax.broadcasted_iota(jnp.int32, (LANES, tq), 1)
    score0 = jnp.where((sid * SEL_BLOCK > qp) | (sid >= n_sel), -3.0,
                       jnp.where((sid == (qp >> 6)) | (sid == 0), FORCE_SCORE, imp))

    score = score0
    for _ in range(top):
        score = jnp.where(score == jnp.max(score, axis=0, keepdims=True), -2.0, score)
    picked = score == -2.0
    mt_ref[0, 0] = jnp.where(picked, 0.0, NEG_BIG).astype(BF16)
    n_picked = jnp.sum(jnp.where(picked, 1.0, 0.0), axis=0, keepdims=True)

    @pl.when(jnp.max(n_picked) > top)
    def _():
        sidf = sid.astype(F32)
        score = score0
        maskt = jnp.full((LANES, tq), NEG_BIG, F32)
        for _ in range(top):
            mx = jnp.max(score, axis=0, keepdims=True)
            idx = jnp.min(jnp.where(score == mx, sidf, float(LANES)), axis=0, keepdims=True)
            pick = (sidf == idx) & (mx > -3.0)
            maskt = jnp.where(pick, 0.0, maskt)
            score = jnp.where(pick, -2.0, score)
        mt_ref[0, 0] = maskt.astype(BF16)


def _cmpsel(qbt, kc, vct, selmap_t):
    B, _, T = qbt.shape
    nr = kc.shape[2]
    tq = NSA_TQ
    n_sel = T // SEL_BLOCK
    return pl.pallas_call(
        functools.partial(_cmpsel_kernel, tq=tq, n_sel=n_sel, top=min(SEL_TOPK, n_sel)),
        grid=(B, NSA_KV_HEADS, T // tq),
        in_specs=[pl.BlockSpec((1, GROUP_W, tq), lambda b, g, i: (b, g, i)),
                  pl.BlockSpec((1, 1, nr, KV_W), lambda b, g, i: (0, b, 0, 0)),
                  pl.BlockSpec((1, 1, HEAD_DIM, nr), lambda b, g, i: (1, b, g, 0)),
                  pl.BlockSpec((LANES, nr), lambda b, g, i: (0, 0))],
        out_specs=[pl.BlockSpec((1, tq, GROUP_W), lambda b, g, i: (b, i, g)),
                   pl.BlockSpec((1, 1, LANES, tq), lambda b, g, i: (b, g, 0, i))],
        out_shape=[jax.ShapeDtypeStruct((B, T, NSA_W), F32),
                   jax.ShapeDtypeStruct((B, NSA_KV_HEADS, LANES, T), BF16)],
        compiler_params=_cparams(("arbitrary", "arbitrary", "arbitrary")),
        name="cmpsel",
    )(qbt, kc, vct, selmap_t)


def _window_attention_t(ws, k_ref, vt_ref, i, tq):
    wlen = WINDOW + tq
    start = pl.multiple_of(jnp.maximum(i * tq - WINDOW, 0), min(tq, WINDOW))
    k = k_ref[0, pl.ds(start, wlen), :]
    vt = _with_ones_rows(vt_ref[0, :, pl.ds(start, wlen)])
    kpos = start + lax.broadcasted_iota(jnp.int32, (wlen, tq), 0)
    qpos = i * tq + lax.broadcasted_iota(jnp.int32, (wlen, tq), 1)
    diff = qpos - kpos
    valid = (diff >= 0) & (diff < WINDOW)
    sts = [_dot(k, w) for w in ws]
    outs = []
    for st in sts:
        st = jnp.where(valid, st, NEG_BIG)
        oe = _dot(vt, jnp.exp2(st - jnp.max(st, axis=0, keepdims=True)).astype(BF16))
        outs.append(oe[0:HEAD_DIM, :] / oe[HEAD_DIM:HEAD_DIM + 1, :])
    return jnp.concatenate(outs, axis=0)


def _sel_kernel(qt_ref, mt_ref, k_ref, vt_ref, kw_ref, vwt_ref, os_ref, ow_ref, w_scr, *scr, tq, tk):
    g = pl.program_id(1)
    i = pl.program_id(2)
    for r, w in enumerate(_group_query_weights(qt_ref[0], g)):
        w_scr[r] = jnp.concatenate([w, mt_ref[0, 0]], axis=0)

    def score_fn(r, start):
        return _dot(k_ref[0, pl.ds(start, tk), :], w_scr[r])

    def vt_fn(r, start):
        return vt_ref[0, :, pl.ds(start, tk)]

    def window():
        return _window_attention_t([w_scr[r, 0:KV_W, :] for r in range(NSA_GROUP)], kw_ref, vwt_ref, i, tq)

    ost, owt = _causal_flash_t(i, NSA_GROUP, tq, tk, score_fn, vt_fn, scr, tail_fn=window)
    os_ref[0] = ost.T
    ow_ref[0] = owt.T


def _sel(qbt, maskt, ks, vst, kw, vwt):
    B, _, T = qbt.shape
    tq, tk = min(SEL_TQ, T), min(SEL_TK, T)
    kaug = KV_W + LANES
    out = pl.BlockSpec((1, tq, GROUP_W), lambda b, g, i: (b, i, g))
    return pl.pallas_call(
        functools.partial(_sel_kernel, tq=tq, tk=tk),
        grid=(B, NSA_KV_HEADS, T // tq),
        in_specs=[pl.BlockSpec((1, GROUP_W, tq), lambda b, g, i: (b, g, i)),
                  pl.BlockSpec((1, 1, LANES, tq), lambda b, g, i: (b, g, 0, i)),
                  pl.BlockSpec((1, T, kaug), lambda b, g, i: (b, 0, 0)),
                  pl.BlockSpec((1, HEAD_DIM, T), lambda b, g, i: (b, g, 0)),
                  pl.BlockSpec((1, T, KV_W), lambda b, g, i: (b, 0, 0)),
                  pl.BlockSpec((1, HEAD_DIM, T), lambda b, g, i: (b, g, 0))],
        out_specs=[out, out],
        out_shape=[jax.ShapeDtypeStruct((B, T, NSA_W), F32)] * 2,
        scratch_shapes=[pltpu.VMEM((NSA_GROUP, kaug, tq), BF16)] + _flash_scratch(NSA_GROUP, tq, tk),
        compiler_params=_cparams(("arbitrary", "arbitrary", "arbitrary")),
        name="sel",
    )(qbt, maskt, ks, vst, kw, vwt)


def _merge_kernel(x_ref, g_ref, oa_ref, oc_ref, os_ref, ow_ref, gb_ref, wg_ref, wof_ref, won_ref, wout_ref, e_ref,
                  o_ref):
    x = x_ref[...]
    h = _rmsnorm(x, g_ref[...]).astype(BF16)
    ya = _dot(oa_ref[...], wof_ref[...])
    sg = jnp.concatenate(_split3(_sigmoid(gb_ref[...])), axis=1)
    on = (_dot(sg, e_ref[0]) * oc_ref[...] + _dot(sg, e_ref[1]) * os_ref[...]
          + _dot(sg, e_ref[2]) * ow_ref[...])
    yb = _dot(on.astype(BF16), won_ref[...])
    mixed = (_sigmoid(_dot(h, wg_ref[:, 0:D_MODEL])) * ya
             + _sigmoid(_dot(h, wg_ref[:, D_MODEL:2 * D_MODEL])) * yb)
    o_ref[...] = x + _dot(mixed.astype(BF16), wout_ref[...])


def _layer_spec(layer, shape):
    return pl.BlockSpec((None,) + shape, lambda *_: (layer,) + (0,) * len(shape))


def _merge(layer, x, g, oa, oc, os_, ow, gb, wp, wof, won, wout, e):
    n = x.shape[0]
    tm = MERGE_TM
    row = lambda w: pl.BlockSpec((tm, w), lambda t: (t, 0))
    const = lambda shape: pl.BlockSpec(shape, lambda t: (0,) * len(shape))
    assert OFF_GATE % (2 * D_MODEL) == 0
    gate_w = pl.BlockSpec((None, D_MODEL, 2 * D_MODEL), lambda t: (layer, 0, OFF_GATE // (2 * D_MODEL)))
    return pl.pallas_call(
        _merge_kernel,
        grid=(n // tm,),
        in_specs=[row(D_MODEL), _layer_spec(layer, (1, D_MODEL)), row(FOX_W), row(NSA_W), row(NSA_W), row(NSA_W),
                  row(LANES), gate_w,
                  _layer_spec(layer, (FOX_W, D_MODEL)), _layer_spec(layer, (NSA_W, D_MODEL)),
                  _layer_spec(layer, (D_MODEL, D_MODEL)), const((3, 3 * LANES, NSA_W))],
        out_specs=row(D_MODEL),
        out_shape=jax.ShapeDtypeStruct((n, D_MODEL), F32),
        compiler_params=_cparams(("arbitrary",)),
        name="merge",
    )(x, g, oa, oc, os_, ow, gb, wp, wof, won, wout, e)


def _mlp_kernel(x_ref, g_ref, wu_ref, wd_ref, gf_ref, o_ref, h_scr, *, final_norm):
    x = x_ref[...]
    h_scr[...] = _rmsnorm(x, g_ref[...]).astype(BF16)
    acc = None
    for c0 in range(0, D_FF, MLP_FF_CHUNK):
        u = jnp.maximum(_dot(h_scr[...], wu_ref[:, c0:c0 + MLP_FF_CHUNK]), 0.0)
        part = _dot((u * u).astype(BF16), wd_ref[c0:c0 + MLP_FF_CHUNK, :])
        acc = part if acc is None else acc + part
    y = x + acc
    o_ref[...] = _rmsnorm(y, gf_ref[...]) if final_norm else y


def _mlp(layer, x, g, wu, wd, gf, final_norm):
    n = x.shape[0]
    tm = MLP_TM
    row = pl.BlockSpec((tm, D_MODEL), lambda t: (t, 0))
    const = lambda shape: pl.BlockSpec(shape, lambda t: (0,) * len(shape))
    return pl.pallas_call(
        functools.partial(_mlp_kernel, final_norm=final_norm),
        grid=(n // tm,),
        in_specs=[row, _layer_spec(layer, (1, D_MODEL)), _layer_spec(layer, (D_MODEL, D_FF)),
                  _layer_spec(layer, (D_FF, D_MODEL)), const((1, D_MODEL))],
        out_specs=row,
        out_shape=jax.ShapeDtypeStruct((n, D_MODEL), F32),
        scratch_shapes=[pltpu.VMEM((tm, D_MODEL), BF16)],
        compiler_params=_cparams(("arbitrary",)),
        name="mlp",
    )(x, g, wu, wd, gf)


def _rope_tables(pos):
    inv = ROPE_THETA ** (-jnp.arange(ROPE_HALF, dtype=F32) / ROPE_HALF)
    ang = pos.astype(F32)[:, None] * inv[None, :]
    cos, sin = jnp.cos(ang), jnp.sin(ang)
    n = pos.shape[0]
    rest = HEAD_DIM - ROPE_DIM
    rc = jnp.concatenate([cos, cos, jnp.ones((n, rest), F32)], axis=1)
    rs1 = jnp.concatenate([-sin, jnp.zeros((n, HEAD_DIM - ROPE_HALF), F32)], axis=1)
    rs2 = jnp.concatenate([jnp.zeros((n, ROPE_HALF), F32), sin, jnp.zeros((n, rest), F32)], axis=1)
    two = lambda t: jnp.concatenate([t, t], axis=1)
    return two(rc), two(rs1), two(rs2)


def _pad_heads(w, n_heads):
    lead = w.shape[:-1]
    w = w.reshape(lead + (n_heads, HEAD_DIM))
    return jnp.concatenate([w, jnp.zeros_like(w)], axis=-1).reshape(lead + (n_heads * LANES,))


def _pad_cols(w, width):
    return jnp.concatenate([w, jnp.zeros(w.shape[:-1] + (width - w.shape[-1],), w.dtype)], axis=-1)


def _proj_weight(w_in):
    o = np.cumsum([0, FOX_W, FOX_W, FOX_W, FOX_HEADS, NSA_W, 6 * KV_W, 3 * NSA_HEADS, D_MODEL, D_MODEL])
    cols = lambda j: w_in[..., o[j]:o[j + 1]]
    qscale = SCALE * LOG2E
    parts = [
        _pad_heads(cols(0) * qscale, FOX_HEADS), _pad_heads(cols(1), FOX_HEADS), cols(2),
        _pad_cols(cols(3), LANES),
        cols(4) * qscale,
        cols(5),
        _pad_cols(cols(6), LANES),
        cols(7), cols(8),
    ]
    return jnp.concatenate(parts, axis=-1).astype(BF16)


def _aug_placement():
    pq = np.zeros((3 * LANES, FOX_PAD_W), np.float32)
    pk = np.zeros((3 * LANES, FOX_PAD_W), np.float32)
    for h in range(FOX_HEADS):
        for piece in range(3):
            pq[piece * LANES + h, h * LANES + AUG_Q_C + piece] = 1.0
            pk[piece * LANES + h, h * LANES + AUG_K_C + piece] = -1.0
    return jnp.asarray(pq, BF16), jnp.asarray(pk, BF16)


def _gate_placement():
    e = np.zeros((3, 3 * LANES, NSA_W), np.float32)
    for j in range(3):
        for h in range(NSA_HEADS):
            for piece in range(3):
                e[j, piece * LANES + 3 * h + j, h * HEAD_DIM:(h + 1) * HEAD_DIM] = 1.0
    return jnp.asarray(e, BF16)


def _sel_map_t(T, nr):
    nc = (T - CMP_BLOCK) // CMP_STRIDE + 1
    n_sel = T // SEL_BLOCK
    cs = np.arange(nc) * CMP_STRIDE
    ce = cs + CMP_BLOCK
    ss = np.arange(n_sel) * SEL_BLOCK
    se = ss + SEL_BLOCK
    ov = np.clip(np.minimum(ce[None, :], se[:, None]) - np.maximum(cs[None, :], ss[:, None]), 0, None)
    m = np.zeros((LANES, nr), np.float32)
    m[:n_sel, :nc] = ov / CMP_BLOCK
    return jnp.asarray(m, BF16)


def _cmp_weights(cmp_pos, cmp_w1, cmp_b1, cmp_w2, cmp_b2):
    d, half, g2 = cmp_w1.shape[0], CMP_BLOCK // 2, NSA_KV_HEADS

    def group_diag(w, axis):
        z = jnp.zeros_like(w)
        return jnp.stack([jnp.concatenate([w, z], axis=-1), jnp.concatenate([z, w], axis=-1)], axis=axis)

    w1 = group_diag(cmp_w1.astype(BF16).reshape(d, 2, 2, half, HEAD_DIM, CMP_HIDDEN), 4)
    w1 = w1.reshape(d, 2, 2, half * KV_W, g2 * CMP_HIDDEN)
    pos = jnp.concatenate([cmp_pos] * g2, axis=-1)
    b1 = jnp.concatenate([cmp_b1] * g2, axis=-1)[:, :, None, :]
    w2 = group_diag(cmp_w2.astype(BF16), 2).reshape(d, 2, g2 * CMP_HIDDEN, KV_W)
    b2 = jnp.concatenate([cmp_b2] * g2, axis=-1)[:, :, None, :]
    return w1, pos, b1, w2, b2


def kernel(x, norm_mix, w_in, b_forget, cmp_pos, cmp_w1, cmp_b1, cmp_w2, cmp_b2, w_o_fox, w_o_nsa, w_out,
           norm_mlp, w_up, w_down, norm_final):
    B, T, _ = x.shape
    depth = w_in.shape[0]
    nr = T // CMP_STRIDE
    n = B * T
    flat = lambda a: a.reshape(n, a.shape[-1])

    pq, pk = _aug_placement()
    gate_e = _gate_placement()
    selmap_t = _sel_map_t(T, nr)
    rope = _rope_tables(jnp.arange(T))
    crope = [jnp.stack([t, jnp.full_like(t, fill)]) for t, fill in
             zip(_rope_tables(jnp.arange(nr) * CMP_STRIDE + CMP_BLOCK - 1), (1.0, 0.0, 0.0))]
    wp = _proj_weight(w_in)
    bf = _pad_cols(b_forget, LANES)[:, None, :]
    cmp_w = _cmp_weights(cmp_pos, cmp_w1, cmp_b1, cmp_w2, cmp_b2)
    wof, won, wout = w_o_fox.astype(BF16), w_o_nsa.astype(BF16), w_out.astype(BF16)
    wu, wd = w_up.astype(BF16), w_down.astype(BF16)
    g_mix, g_mlp = norm_mix[:, None, :], norm_mlp[:, None, :]

    for l in range(depth):
        qat, ka, vat, qbt, ks, kw, vst, vwt, kvc, gb = _proj(l, x, g_mix, wp, bf, pq, pk, *rope)
        oa = _fox(qat, ka, vat)
        cmp_rows, cmp_cols = _cmp(l, kvc, *cmp_w, *crope)
        oc, maskt = _cmpsel(qbt, cmp_rows, cmp_cols, selmap_t)
        os_, ow = _sel(qbt, maskt, ks, vst, kw, vwt)
        x1 = _merge(l, flat(x), g_mix, flat(oa), flat(oc), flat(os_), flat(ow), flat(gb), wp,
                    wof, won, wout, gate_e)
        x = _mlp(l, x1, g_mlp, wu, wd, norm_final[None, :], l == depth - 1).reshape(B, T, D_MODEL)
    return x
```

```python
import functools

import numpy as np
import jax
import jax.numpy as jnp
from jax import lax
from jax.experimental import pallas as pl
from jax.experimental.pallas import tpu as pltpu

F32 = jnp.float32
BF16 = jnp.bfloat16

D_MODEL = 1024
HEAD_DIM = 64
FOX_HEADS = 8
NSA_HEADS = 8
NSA_KV_HEADS = 2
NSA_GROUP = NSA_HEADS // NSA_KV_HEADS
CMP_BLOCK = 32
CMP_STRIDE = 16
CMP_HIDDEN = 4 * HEAD_DIM
SEL_BLOCK = 64
SEL_TOPK = 16
WINDOW = 512
ROPE_THETA = 500000.0
ROPE_DIM = HEAD_DIM // 4
ROPE_HALF = ROPE_DIM // 2
D_FF = 4 * D_MODEL
RMS_EPS = 1e-6
FORCE_SCORE = 1e6
NEG_BIG = -1e30
SCALE = HEAD_DIM ** -0.5
LOG2E = 1.4426950408889634

FOX_W = FOX_HEADS * HEAD_DIM
NSA_W = NSA_HEADS * HEAD_DIM
KV_W = NSA_KV_HEADS * HEAD_DIM

LANES = 128
DEN_ROWS = 16
VMEM_LIMIT_BYTES = 56 * 1024 * 1024

FOX_PAD_W = FOX_HEADS * LANES
GROUP_W = NSA_GROUP * HEAD_DIM
OFF_QA = 0
OFF_KA = OFF_QA + FOX_PAD_W
OFF_VA = OFF_KA + FOX_PAD_W
OFF_F = OFF_VA + FOX_W
OFF_QB = OFF_F + LANES
OFF_KC = OFF_QB + NSA_W
OFF_VC = OFF_KC + KV_W
OFF_KS = OFF_VC + KV_W
OFF_VS = OFF_KS + KV_W
OFF_KW = OFF_VS + KV_W
OFF_VW = OFF_KW + KV_W
OFF_GB = OFF_VW + KV_W
OFF_GATE = OFF_GB + LANES
PROJ_W = OFF_GATE + 2 * D_MODEL

AUG_Q_C = HEAD_DIM
AUG_K_C = HEAD_DIM + 3

PROJ_TM = 256
FOX_TQ = 512
FOX_TK = 512
FOX_HEADS_PER_STEP = 4
NSA_TQ = 256
SEL_TQ = 512
SEL_TK = 512
MERGE_TM = 256
MLP_TM = 512
MLP_FF_CHUNK = 512


def _cparams(sem):
    return pltpu.CompilerParams(dimension_semantics=sem, vmem_limit_bytes=VMEM_LIMIT_BYTES)


def _dot(a, b):
    return jnp.dot(a, b, preferred_element_type=F32)


def _split3(x):
    hi = x.astype(BF16)
    r = x - hi.astype(F32)
    mid = r.astype(BF16)
    lo = (r - mid.astype(F32)).astype(BF16)
    return hi, mid, lo


def _lane_tile(x, reps):
    return x if reps == 1 else jnp.concatenate([x] * reps, axis=1)


def _rope(y, rc, rs1, rs2):
    n = y.shape[1]
    return y * rc + pltpu.roll(y, n - ROPE_HALF, 1) * rs1 + pltpu.roll(y, ROPE_HALF, 1) * rs2


def _rmsnorm(x, g):
    return x * lax.rsqrt(jnp.mean(x * x, axis=-1, keepdims=True) + RMS_EPS) * g


def _sigmoid(x):
    return 1.0 / (1.0 + jnp.exp(-x))


def _proj_kernel(x_ref, g_ref, w_ref, bf_ref, pq_ref, pk_ref, rc_ref, rs1_ref, rs2_ref,
                 qa_ref, ka_ref, va_ref, qb_ref, ks_ref, kw_ref, vs_ref, vw_ref, kvc_ref, gb_ref,
                 h_scr, carry_scr, *, tm):
    t = pl.program_id(1)

    @pl.when(t == 0)
    def _():
        carry_scr[...] = jnp.zeros_like(carry_scr)

    h_scr[...] = _rmsnorm(x_ref[0], g_ref[...]).astype(BF16)

    def seg(c0, n):
        return _dot(h_scr[...], w_ref[:, c0:c0 + n])

    f = seg(OFF_F, LANES) + bf_ref[...]
    lf = jnp.minimum(f, 0.0) - jnp.log1p(jnp.exp(-jnp.abs(f)))
    row = lax.broadcasted_iota(jnp.int32, (tm, tm), 0)
    col = lax.broadcasted_iota(jnp.int32, (tm, tm), 1)
    tri = jnp.where(col <= row, 1.0, 0.0).astype(BF16)
    hi, mid, lo = _split3(lf)
    c = _dot(tri, hi) + _dot(tri, mid) + _dot(tri, lo) + carry_scr[0:1, :]
    carry_scr[0:1, :] = c[tm - 1:tm, :]
    c3 = jnp.concatenate(_split3(c * LOG2E), axis=1)

    lane = lax.broadcasted_iota(jnp.int32, (tm, FOX_PAD_W), 1) & (LANES - 1)
    ones_q = jnp.where((lane >= AUG_K_C) & (lane < AUG_K_C + 3), 1.0, 0.0)
    ones_k = jnp.where((lane >= AUG_Q_C) & (lane < AUG_Q_C + 3), 1.0, 0.0)
    qa_ref[0] = (seg(OFF_QA, FOX_PAD_W) + _dot(c3, pq_ref[...]) + ones_q).T.astype(BF16)
    ka_ref[0] = (seg(OFF_KA, FOX_PAD_W) + _dot(c3, pk_ref[...]) + ones_k).astype(BF16)
    va_ref[0] = seg(OFF_VA, FOX_W).T.astype(BF16)

    rc, rs1, rs2 = rc_ref[...], rs1_ref[...], rs2_ref[...]
    reps = NSA_W // LANES
    qb_ref[0] = _rope(seg(OFF_QB, NSA_W), _lane_tile(rc, reps), _lane_tile(rs1, reps),
                      _lane_tile(rs2, reps)).T.astype(BF16)
    pos = t * tm + lax.broadcasted_iota(jnp.int32, (tm, LANES), 0)
    onehot = jnp.where((pos >> 6) == lax.broadcasted_iota(jnp.int32, (tm, LANES), 1), 1.0, 0.0)
    ks_ref[0] = jnp.concatenate([_rope(seg(OFF_KS, KV_W), rc, rs1, rs2), onehot], axis=1).astype(BF16)
    kw_ref[0] = _rope(seg(OFF_KW, KV_W), rc, rs1, rs2).astype(BF16)
    vs_ref[0] = seg(OFF_VS, KV_W).T.astype(BF16)
    vw_ref[0] = seg(OFF_VW, KV_W).T.astype(BF16)
    kvc_ref[0, 0] = seg(OFF_KC, KV_W)
    kvc_ref[1, 0] = seg(OFF_VC, KV_W)
    gb_ref[0] = seg(OFF_GB, LANES)


def _proj(layer, x, g, wp, bf, pq, pk, rc, rs1, rs2):
    B, T, _ = x.shape
    tm = PROJ_TM
    row = lambda w: pl.BlockSpec((1, tm, w), lambda b, t: (b, t, 0))
    col = lambda w: pl.BlockSpec((1, w, tm), lambda b, t: (b, 0, t))
    const = lambda shape: pl.BlockSpec(shape, lambda b, t: (0,) * len(shape))
    tab = pl.BlockSpec((tm, LANES), lambda b, t: (t, 0))
    out_shapes = [
        jax.ShapeDtypeStruct((B, FOX_PAD_W, T), BF16),
        jax.ShapeDtypeStruct((B, T, FOX_PAD_W), BF16),
        jax.ShapeDtypeStruct((B, FOX_W, T), BF16),
        jax.ShapeDtypeStruct((B, NSA_W, T), BF16),
        jax.ShapeDtypeStruct((B, T, KV_W + LANES), BF16),
        jax.ShapeDtypeStruct((B, T, KV_W), BF16),
        jax.ShapeDtypeStruct((B, KV_W, T), BF16),
        jax.ShapeDtypeStruct((B, KV_W, T), BF16),
        jax.ShapeDtypeStruct((2, B, T, KV_W), F32),
        jax.ShapeDtypeStruct((B, T, LANES), F32),
    ]
    out_specs = [
        col(FOX_PAD_W), row(FOX_PAD_W), col(FOX_W), col(NSA_W),
        row(KV_W + LANES), row(KV_W), col(KV_W), col(KV_W),
        pl.BlockSpec((2, 1, tm, KV_W), lambda b, t: (0, b, t, 0)),
        row(LANES),
    ]
    return pl.pallas_call(
        functools.partial(_proj_kernel, tm=tm),
        grid=(B, T // tm),
        in_specs=[row(D_MODEL), _layer_spec(layer, (1, D_MODEL)), _layer_spec(layer, (D_MODEL, OFF_GATE)),
                  _layer_spec(layer, (1, LANES)),
                  const((3 * LANES, FOX_PAD_W)), const((3 * LANES, FOX_PAD_W)), tab, tab, tab],
        out_specs=out_specs,
        out_shape=out_shapes,
        scratch_shapes=[pltpu.VMEM((tm, D_MODEL), BF16), pltpu.VMEM((8, LANES), F32)],
        compiler_params=_cparams(("arbitrary", "arbitrary")),
        name="proj",
    )(x, g, wp, bf, pq, pk, rc, rs1, rs2)


def _with_ones_rows(vt):
    first = lax.broadcasted_iota(jnp.int32, (DEN_ROWS, vt.shape[1]), 0) == 0
    return jnp.concatenate([vt, jnp.where(first, 1.0, 0.0).astype(vt.dtype)], axis=0)


def _causal_mask_t(st, i, j, tq, tk):
    kpos = j * tk + lax.broadcasted_iota(jnp.int32, (tk, tq), 0)
    qpos = i * tq + lax.broadcasted_iota(jnp.int32, (tk, tq), 1)
    return jnp.where(kpos <= qpos, st, NEG_BIG)


def _causal_flash_t(i, n_heads, tq, tk, score_fn, vt_fn, scr, window=None):
    s0_scr, s1_scr, m_scr, acc_scr = scr
    n_full = (i * tq) // tk

    def scores(j, s_scr, fn=score_fn):
        start = pl.multiple_of(j * tk, tk)
        for h in range(n_heads):
            s_scr[h] = fn(h, start)

    def softmax_pv(j, s_scr, mask=None, vt=vt_fn, m_ref=m_scr, acc_ref=acc_scr):
        start = pl.multiple_of(j * tk, tk)
        for h in range(n_heads):
            st = s_scr[h] if mask is None else mask(s_scr[h])
            m = m_ref[h]
            m_new = jnp.maximum(m, jnp.max(st, axis=0, keepdims=True))
            p = jnp.exp2(st - m_new).astype(BF16)
            acc_ref[h] = jnp.exp2(m - m_new) * acc_ref[h] + _dot(_with_ones_rows(vt(h, start)), p)
            m_ref[h] = m_new

    def normalised(acc_ref):
        return jnp.concatenate([acc_ref[h, 0:HEAD_DIM, :] / acc_ref[h, HEAD_DIM:HEAD_DIM + 1, :]
                                for h in range(n_heads)], axis=0)

    causal = lambda st: _causal_mask_t(st, i, n_full, tq, tk)
    m_scr[...] = jnp.full_like(m_scr, NEG_BIG)
    acc_scr[...] = jnp.zeros_like(acc_scr)

    odd = n_full & 1

    @pl.when(odd == 1)
    def _():
        scores(0, s1_scr)
        scores(1, s0_scr)
        softmax_pv(0, s1_scr)

    @pl.when(odd == 0)
    def _():
        scores(0, s0_scr)

    def body(jp, carry):
        t0 = odd + 2 * jp
        scores(t0 + 1, s1_scr)
        softmax_pv(t0, s0_scr)
        scores(t0 + 2, s0_scr)
        softmax_pv(t0 + 1, s1_scr)
        return carry

    lax.fori_loop(0, n_full // 2, body, 0)
    if window is None:
        softmax_pv(n_full, s0_scr, causal)
        return normalised(acc_scr), None

    assert tq == tk == WINDOW
    w_score, w_vt, wm_scr, wacc_scr = window
    wm_scr[...] = jnp.full_like(wm_scr, NEG_BIG)
    wacc_scr[...] = jnp.zeros_like(wacc_scr)
    prev = jnp.maximum(i - 1, 0)
    first_row = jnp.where(i > 0, 0, tk + tq)

    def in_window(st):
        r = lax.broadcasted_iota(jnp.int32, (tk, tq), 0)
        c = lax.broadcasted_iota(jnp.int32, (tk, tq), 1)
        return jnp.where(r - c > first_row, st, NEG_BIG)

    stats = dict(vt=w_vt, m_ref=wm_scr, acc_ref=wacc_scr)
    scores(i, s1_scr, w_score)
    softmax_pv(n_full, s0_scr, causal)
    scores(prev, s0_scr, w_score)
    softmax_pv(i, s1_scr, causal, **stats)
    softmax_pv(prev, s0_scr, in_window, **stats)
    return normalised(acc_scr), normalised(wacc_scr)


def _flash_scratch(n_heads, tq, tk):
    return [pltpu.VMEM((n_heads, tk, tq), F32), pltpu.VMEM((n_heads, tk, tq), F32),
            pltpu.VMEM((n_heads, 1, tq), F32), pltpu.VMEM((n_heads, HEAD_DIM + DEN_ROWS, tq), F32)]


def _fox_kernel(qt_ref, k_ref, vt_ref, o_ref, *scr, tq, tk, heads):
    def score_fn(h, start):
        return _dot(k_ref[0, pl.ds(start, tk), h * LANES:(h + 1) * LANES], qt_ref[0, h * LANES:(h + 1) * LANES, :])

    def vt_fn(h, start):
        return vt_ref[0, h * HEAD_DIM:(h + 1) * HEAD_DIM, pl.ds(start, tk)]

    ot, _ = _causal_flash_t(pl.program_id(2), heads, tq, tk, score_fn, vt_fn, scr)
    o_ref[0] = ot.T.astype(BF16)


def _fox(qat, ka, vat):
    B, T, _ = ka.shape
    tq, tk = min(FOX_TQ, T), min(FOX_TK, T)
    heads = FOX_HEADS_PER_STEP
    return pl.pallas_call(
        functools.partial(_fox_kernel, tq=tq, tk=tk, heads=heads),
        grid=(B, FOX_HEADS // heads, T // tq),
        in_specs=[pl.BlockSpec((1, heads * LANES, tq), lambda b, h, i: (b, h, i)),
                  pl.BlockSpec((1, T, heads * LANES), lambda b, h, i: (b, 0, h)),
                  pl.BlockSpec((1, heads * HEAD_DIM, T), lambda b, h, i: (b, h, 0))],
        out_specs=pl.BlockSpec((1, tq, heads * HEAD_DIM), lambda b, h, i: (b, i, h)),
        out_shape=jax.ShapeDtypeStruct((B, T, FOX_W), BF16),
        scratch_shapes=_flash_scratch(heads, tq, tk),
        compiler_params=_cparams(("arbitrary", "arbitrary", "arbitrary")),
        name="fox",
    )(qat, ka, vat)


def _cmp_kernel(r_ref, w1_ref, pos_ref, b1_ref, w2_ref, b2_ref, rc_ref, rs1_ref, rs2_ref, o_ref, ot_ref):
    nr = o_ref.shape[0]
    half = CMP_BLOCK // 2
    a = bm = None
    for l in range(half):
        x = r_ref[pl.ds(l, nr, stride=CMP_STRIDE), :]
        ta = _dot((x + pos_ref[l:l + 1, :]).astype(BF16), w1_ref[0, l * KV_W:(l + 1) * KV_W, :])
        tb = _dot((x + pos_ref[half + l:half + l + 1, :]).astype(BF16), w1_ref[1, l * KV_W:(l + 1) * KV_W, :])
        a = ta if a is None else a + ta
        bm = tb if bm is None else bm + tb
    hdn = a + pltpu.roll(bm, nr - 1, 0) + b1_ref[...]
    act = 0.5 * hdn * (1.0 + jnp.tanh(0.7978845608028654 * (hdn + 0.044715 * (hdn * hdn * hdn))))
    y = _rope(_dot(act.astype(BF16), w2_ref[...]) + b2_ref[...], rc_ref[...], rs1_ref[...], rs2_ref[...])
    o_ref[...] = y.astype(BF16)
    ot_ref[...] = y.T.astype(BF16)


def _cmp(layer, kvc, w1, pos, b1, w2, b2, rc, rs1, rs2):
    _, B, T, _ = kvc.shape
    nr = T // CMP_STRIDE
    rw = CMP_STRIDE * KV_W
    per_kv = lambda shape: pl.BlockSpec((None, None) + shape, lambda s, b: (layer, s) + (0,) * len(shape))
    tab = pl.BlockSpec((None, nr, LANES), lambda s, b: (s, 0, 0))
    return pl.pallas_call(
        _cmp_kernel,
        grid=(2, B),
        in_specs=[pl.BlockSpec((None, None, T, KV_W), lambda s, b: (s, b, 0, 0)),
                  per_kv((2, rw, 2 * CMP_HIDDEN)), per_kv((CMP_BLOCK, KV_W)), per_kv((1, 2 * CMP_HIDDEN)),
                  per_kv((2 * CMP_HIDDEN, KV_W)), per_kv((1, KV_W)), tab, tab, tab],
        out_specs=[pl.BlockSpec((None, None, nr, KV_W), lambda s, b: (s, b, 0, 0)),
                   pl.BlockSpec((None, None, KV_W, nr), lambda s, b: (s, b, 0, 0))],
        out_shape=[jax.ShapeDtypeStruct((2, B, nr, KV_W), BF16), jax.ShapeDtypeStruct((2, B, KV_W, nr), BF16)],
        compiler_params=_cparams(("arbitrary", "arbitrary")),
        name="cmp",
    )(kvc, w1, pos, b1, w2, b2, rc, rs1, rs2)


def _group_query_weights(qt, g):
    zero = jnp.zeros((HEAD_DIM, qt.shape[1]), qt.dtype)
    out = []
    for r in range(NSA_GROUP):
        q = qt[r * HEAD_DIM:(r + 1) * HEAD_DIM, :]
        out.append(jnp.concatenate([jnp.where(g == 0, q, zero), jnp.where(g == 1, q, zero)], axis=0))
    return out


def _cmpsel_kernel(qt_ref, kc_ref, vct_ref, mapt_ref, oc_ref, mt_ref, *, tq, n_sel, top):
    g = pl.program_id(1)
    i = pl.program_id(2)
    nr = kc_ref.shape[2]
    kc = kc_ref[0, 0]
    vct = vct_ref[0, 0]
    cend = lax.broadcasted_iota(jnp.int32, (nr, tq), 0) * CMP_STRIDE + (CMP_BLOCK - 1)
    qpos = i * tq + lax.broadcasted_iota(jnp.int32, (nr, tq), 1)
    valid = cend <= qpos
    sts = [_dot(kc, w) for w in _group_query_weights(qt_ref[0], g)]
    vct_ones = _with_ones_rows(vct)
    imp = jnp.zeros((LANES, tq), F32)
    outs = []
    for st in sts:
        st = jnp.where(valid, st, NEG_BIG)
        e = jnp.where(valid, jnp.exp2(st - jnp.max(st, axis=0, keepdims=True)), 0.0).astype(BF16)
        oe = _dot(vct_ones, e)
        den = oe[HEAD_DIM:HEAD_DIM + 1, :]
        inv = jnp.where(den > 0.0, 1.0 / jnp.where(den > 0.0, den, 1.0), 0.0)
        outs.append(oe[0:HEAD_DIM, :] * inv)
        imp = imp + _dot(mapt_ref[...], e) * inv
    oc_ref[0] = jnp.concatenate(outs, axis=0).T

    sid = lax.broadcasted_iota(jnp.int32, (LANES, tq), 0)
    qp = i * tq + lax.broadcasted_iota(jnp.int32, (LANES, tq), 1)
    score0 = jnp.where((sid * SEL_BLOCK > qp) | (sid >= n_sel), -3.0,
                       jnp.where(sid == (qp >> 6), FORCE_SCORE, jnp.where(sid == 0, 2.0 * FORCE_SCORE, imp)))

    score = score0
    for _ in range(top):
        score = jnp.where(score == jnp.max(score, axis=0, keepdims=True), -2.0, score)
    picked = score == -2.0
    mt_ref[0, 0] = jnp.where(picked, 0.0, NEG_BIG).astype(BF16)
    n_picked = jnp.sum(jnp.where(picked, 1.0, 0.0), axis=0, keepdims=True)

    @pl.when(jnp.max(n_picked) > top)
    def _():
        sidf = sid.astype(F32)
        score = score0
        maskt = jnp.full((LANES, tq), NEG_BIG, F32)
        for _ in range(top):
            mx = jnp.max(score, axis=0, keepdims=True)
            idx = jnp.min(jnp.where(score == mx, sidf, float(LANES)), axis=0, keepdims=True)
            pick = (sidf == idx) & (mx > -3.0)
            maskt = jnp.where(pick, 0.0, maskt)
            score = jnp.where(pick, -2.0, score)
        mt_ref[0, 0] = maskt.astype(BF16)


def _cmpsel(qbt, kc, vct, selmap_t):
    B, _, T = qbt.shape
    nr = kc.shape[2]
    tq = NSA_TQ
    n_sel = T // SEL_BLOCK
    return pl.pallas_call(
        functools.partial(_cmpsel_kernel, tq=tq, n_sel=n_sel, top=min(SEL_TOPK, n_sel)),
        grid=(B, NSA_KV_HEADS, T // tq),
        in_specs=[pl.BlockSpec((1, GROUP_W, tq), lambda b, g, i: (b, g, i)),
                  pl.BlockSpec((1, 1, nr, KV_W), lambda b, g, i: (0, b, 0, 0)),
                  pl.BlockSpec((1, 1, HEAD_DIM, nr), lambda b, g, i: (1, b, g, 0)),
                  pl.BlockSpec((LANES, nr), lambda b, g, i: (0, 0))],
        out_specs=[pl.BlockSpec((1, tq, GROUP_W), lambda b, g, i: (b, i, g)),
                   pl.BlockSpec((1, 1, LANES, tq), lambda b, g, i: (b, g, 0, i))],
        out_shape=[jax.ShapeDtypeStruct((B, T, NSA_W), F32),
                   jax.ShapeDtypeStruct((B, NSA_KV_HEADS, LANES, T), BF16)],
        compiler_params=_cparams(("arbitrary", "arbitrary", "arbitrary")),
        name="cmpsel",
    )(qbt, kc, vct, selmap_t)


def _sel_kernel(qt_ref, mt_ref, k_ref, vt_ref, kw_ref, vwt_ref, os_ref, ow_ref, w_scr, wm_scr, wacc_scr, *scr,
                tq, tk):
    g = pl.program_id(1)
    for r, w in enumerate(_group_query_weights(qt_ref[0], g)):
        w_scr[r] = jnp.concatenate([w, mt_ref[0, 0]], axis=0)

    def score_fn(r, start):
        return _dot(k_ref[0, pl.ds(start, tk), :], w_scr[r])

    def vt_fn(r, start):
        return vt_ref[0, :, pl.ds(start, tk)]

    def w_score_fn(r, start):
        return _dot(kw_ref[0, pl.ds(start, tk), :], w_scr[r, 0:KV_W, :])

    def w_vt_fn(r, start):
        return vwt_ref[0, :, pl.ds(start, tk)]

    ost, owt = _causal_flash_t(pl.program_id(2), NSA_GROUP, tq, tk, score_fn, vt_fn, scr,
                               window=(w_score_fn, w_vt_fn, wm_scr, wacc_scr))
    os_ref[0] = ost.T
    ow_ref[0] = owt.T


def _sel(qbt, maskt, ks, vst, kw, vwt):
    B, _, T = qbt.shape
    tq, tk = min(SEL_TQ, T), min(SEL_TK, T)
    kaug = KV_W + LANES
    out = pl.BlockSpec((1, tq, GROUP_W), lambda b, g, i: (b, i, g))
    return pl.pallas_call(
        functools.partial(_sel_kernel, tq=tq, tk=tk),
        grid=(B, NSA_KV_HEADS, T // tq),
        in_specs=[pl.BlockSpec((1, GROUP_W, tq), lambda b, g, i: (b, g, i)),
                  pl.BlockSpec((1, 1, LANES, tq), lambda b, g, i: (b, g, 0, i)),
                  pl.BlockSpec((1, T, kaug), lambda b, g, i: (b, 0, 0)),
                  pl.BlockSpec((1, HEAD_DIM, T), lambda b, g, i: (b, g, 0)),
                  pl.BlockSpec((1, T, KV_W), lambda b, g, i: (b, 0, 0)),
                  pl.BlockSpec((1, HEAD_DIM, T), lambda b, g, i: (b, g, 0))],
        out_specs=[out, out],
        out_shape=[jax.ShapeDtypeStruct((B, T, NSA_W), F32)] * 2,
        scratch_shapes=[pltpu.VMEM((NSA_GROUP, kaug, tq), BF16)] + _flash_scratch(NSA_GROUP, tq, tk)[2:]
        + _flash_scratch(NSA_GROUP, tq, tk),
        compiler_params=_cparams(("arbitrary", "arbitrary", "arbitrary")),
        name="sel",
    )(qbt, maskt, ks, vst, kw, vwt)


def _merge_kernel(x_ref, g_ref, oa_ref, oc_ref, os_ref, ow_ref, gb_ref, wg_ref, wof_ref, won_ref, wout_ref, e_ref,
                  o_ref):
    x = x_ref[...]
    h = _rmsnorm(x, g_ref[...]).astype(BF16)
    ya = _dot(oa_ref[...], wof_ref[...])
    sg = jnp.concatenate(_split3(_sigmoid(gb_ref[...])), axis=1)
    on = (_dot(sg, e_ref[0]) * oc_ref[...] + _dot(sg, e_ref[1]) * os_ref[...]
          + _dot(sg, e_ref[2]) * ow_ref[...])
    yb = _dot(on.astype(BF16), won_ref[...])
    mixed = (_sigmoid(_dot(h, wg_ref[:, 0:D_MODEL])) * ya
             + _sigmoid(_dot(h, wg_ref[:, D_MODEL:2 * D_MODEL])) * yb)
    o_ref[...] = x + _dot(mixed.astype(BF16), wout_ref[...])


def _layer_spec(layer, shape):
    return pl.BlockSpec((None,) + shape, lambda *_: (layer,) + (0,) * len(shape))


def _merge(layer, x, g, oa, oc, os_, ow, gb, wp, wof, won, wout, e):
    n = x.shape[0]
    tm = MERGE_TM
    row = lambda w: pl.BlockSpec((tm, w), lambda t: (t, 0))
    const = lambda shape: pl.BlockSpec(shape, lambda t: (0,) * len(shape))
    assert OFF_GATE % (2 * D_MODEL) == 0
    gate_w = pl.BlockSpec((None, D_MODEL, 2 * D_MODEL), lambda t: (layer, 0, OFF_GATE // (2 * D_MODEL)))
    return pl.pallas_call(
        _merge_kernel,
        grid=(n // tm,),
        in_specs=[row(D_MODEL), _layer_spec(layer, (1, D_MODEL)), row(FOX_W), row(NSA_W), row(NSA_W), row(NSA_W),
                  row(LANES), gate_w,
                  _layer_spec(layer, (FOX_W, D_MODEL)), _layer_spec(layer, (NSA_W, D_MODEL)),
                  _layer_spec(layer, (D_MODEL, D_MODEL)), const((3, 3 * LANES, NSA_W))],
        out_specs=row(D_MODEL),
        out_shape=jax.ShapeDtypeStruct((n, D_MODEL), F32),
        compiler_params=_cparams(("arbitrary",)),
        name="merge",
    )(x, g, oa, oc, os_, ow, gb, wp, wof, won, wout, e)


def _mlp_kernel(x_ref, g_ref, wu_ref, wd_ref, gf_ref, o_ref, h_scr, *, final_norm):
    x = x_ref[...]
    h_scr[...] = _rmsnorm(x, g_ref[...]).astype(BF16)
    acc = None
    for c0 in range(0, D_FF, MLP_FF_CHUNK):
        u = jnp.maximum(_dot(h_scr[...], wu_ref[:, c0:c0 + MLP_FF_CHUNK]), 0.0)
        part = _dot((u * u).astype(BF16), wd_ref[c0:c0 + MLP_FF_CHUNK, :])
        acc = part if acc is None else acc + part
    y = x + acc
    o_ref[...] = _rmsnorm(y, gf_ref[...]) if final_norm else y


def _mlp(layer, x, g, wu, wd, gf, final_norm):
    n = x.shape[0]
    tm = MLP_TM
    row = pl.BlockSpec((tm, D_MODEL), lambda t: (t, 0))
    const = lambda shape: pl.BlockSpec(shape, lambda t: (0,) * len(shape))
    return pl.pallas_call(
        functools.partial(_mlp_kernel, final_norm=final_norm),
        grid=(n // tm,),
        in_specs=[row, _layer_spec(layer, (1, D_MODEL)), _layer_spec(layer, (D_MODEL, D_FF)),
                  _layer_spec(layer, (D_FF, D_MODEL)), const((1, D_MODEL))],
        out_specs=row,
        out_shape=jax.ShapeDtypeStruct((n, D_MODEL), F32),
        scratch_shapes=[pltpu.VMEM((tm, D_MODEL), BF16)],
        compiler_params=_cparams(("arbitrary",)),
        name="mlp",
    )(x, g, wu, wd, gf)


def _rope_tables(pos):
    inv = ROPE_THETA ** (-jnp.arange(ROPE_HALF, dtype=F32) / ROPE_HALF)
    ang = pos.astype(F32)[:, None] * inv[None, :]
    cos, sin = jnp.cos(ang), jnp.sin(ang)
    n = pos.shape[0]
    rest = HEAD_DIM - ROPE_DIM
    rc = jnp.concatenate([cos, cos, jnp.ones((n, rest), F32)], axis=1)
    rs1 = jnp.concatenate([-sin, jnp.zeros((n, HEAD_DIM - ROPE_HALF), F32)], axis=1)
    rs2 = jnp.concatenate([jnp.zeros((n, ROPE_HALF), F32), sin, jnp.zeros((n, rest), F32)], axis=1)
    two = lambda t: jnp.concatenate([t, t], axis=1)
    return two(rc), two(rs1), two(rs2)


def _pad_heads(w, n_heads):
    lead = w.shape[:-1]
    w = w.reshape(lead + (n_heads, HEAD_DIM))
    return jnp.concatenate([w, jnp.zeros_like(w)], axis=-1).reshape(lead + (n_heads * LANES,))


def _pad_cols(w, width):
    return jnp.concatenate([w, jnp.zeros(w.shape[:-1] + (width - w.shape[-1],), w.dtype)], axis=-1)


def _proj_weight(w_in):
    o = np.cumsum([0, FOX_W, FOX_W, FOX_W, FOX_HEADS, NSA_W, 6 * KV_W, 3 * NSA_HEADS, D_MODEL, D_MODEL])
    cols = lambda j: w_in[..., o[j]:o[j + 1]]
    qscale = SCALE * LOG2E
    parts = [
        _pad_heads(cols(0) * qscale, FOX_HEADS), _pad_heads(cols(1), FOX_HEADS), cols(2),
        _pad_cols(cols(3), LANES),
        cols(4) * qscale,
        cols(5),
        _pad_cols(cols(6), LANES),
        cols(7), cols(8),
    ]
    return jnp.concatenate(parts, axis=-1).astype(BF16)


def _aug_placement():
    pq = np.zeros((3 * LANES, FOX_PAD_W), np.float32)
    pk = np.zeros((3 * LANES, FOX_PAD_W), np.float32)
    for h in range(FOX_HEADS):
        for piece in range(3):
            pq[piece * LANES + h, h * LANES + AUG_Q_C + piece] = 1.0
            pk[piece * LANES + h, h * LANES + AUG_K_C + piece] = -1.0
    return jnp.asarray(pq, BF16), jnp.asarray(pk, BF16)


def _gate_placement():
    e = np.zeros((3, 3 * LANES, NSA_W), np.float32)
    for j in range(3):
        for h in range(NSA_HEADS):
            for piece in range(3):
                e[j, piece * LANES + 3 * h + j, h * HEAD_DIM:(h + 1) * HEAD_DIM] = 1.0
    return jnp.asarray(e, BF16)


def _sel_map_t(T, nr):
    nc = (T - CMP_BLOCK) // CMP_STRIDE + 1
    n_sel = T // SEL_BLOCK
    cs = np.arange(nc) * CMP_STRIDE
    ce = cs + CMP_BLOCK
    ss = np.arange(n_sel) * SEL_BLOCK
    se = ss + SEL_BLOCK
    ov = np.clip(np.minimum(ce[None, :], se[:, None]) - np.maximum(cs[None, :], ss[:, None]), 0, None)
    m = np.zeros((LANES, nr), np.float32)
    m[:n_sel, :nc] = ov / CMP_BLOCK
    return jnp.asarray(m, BF16)


def _cmp_weights(cmp_pos, cmp_w1, cmp_b1, cmp_w2, cmp_b2):
    d, half, g2 = cmp_w1.shape[0], CMP_BLOCK // 2, NSA_KV_HEADS

    def group_diag(w, axis):
        z = jnp.zeros_like(w)
        return jnp.stack([jnp.concatenate([w, z], axis=-1), jnp.concatenate([z, w], axis=-1)], axis=axis)

    w1 = group_diag(cmp_w1.astype(BF16).reshape(d, 2, 2, half, HEAD_DIM, CMP_HIDDEN), 4)
    w1 = w1.reshape(d, 2, 2, half * KV_W, g2 * CMP_HIDDEN)
    pos = jnp.concatenate([cmp_pos] * g2, axis=-1)
    b1 = jnp.concatenate([cmp_b1] * g2, axis=-1)[:, :, None, :]
    w2 = group_diag(cmp_w2.astype(BF16), 2).reshape(d, 2, g2 * CMP_HIDDEN, KV_W)
    b2 = jnp.concatenate([cmp_b2] * g2, axis=-1)[:, :, None, :]
    return w1, pos, b1, w2, b2


def kernel(x, norm_mix, w_in, b_forget, cmp_pos, cmp_w1, cmp_b1, cmp_w2, cmp_b2, w_o_fox, w_o_nsa, w_out,
           norm_mlp, w_up, w_down, norm_final):
    B, T, _ = x.shape
    depth = w_in.shape[0]
    nr = T // CMP_STRIDE
    n = B * T
    flat = lambda a: a.reshape(n, a.shape[-1])

    pq, pk = _aug_placement()
    gate_e = _gate_placement()
    selmap_t = _sel_map_t(T, nr)
    rope = _rope_tables(jnp.arange(T))
    crope = [jnp.stack([t, jnp.full_like(t, fill)]) for t, fill in
             zip(_rope_tables(jnp.arange(nr) * CMP_STRIDE + CMP_BLOCK - 1), (1.0, 0.0, 0.0))]
    wp = _proj_weight(w_in)
    bf = _pad_cols(b_forget, LANES)[:, None, :]
    cmp_w = _cmp_weights(cmp_pos, cmp_w1, cmp_b1, cmp_w2, cmp_b2)
    wof, won, wout = w_o_fox.astype(BF16), w_o_nsa.astype(BF16), w_out.astype(BF16)
    wu, wd = w_up.astype(BF16), w_down.astype(BF16)
    g_mix, g_mlp = norm_mix[:, None, :], norm_mlp[:, None, :]

    for l in range(depth):
        qat, ka, vat, qbt, ks, kw, vst, vwt, kvc, gb = _proj(l, x, g_mix, wp, bf, pq, pk, *rope)
        oa = _fox(qat, ka, vat)
        cmp_rows, cmp_cols = _cmp(l, kvc, *cmp_w, *crope)
        oc, maskt = _cmpsel(qbt, cmp_rows, cmp_cols, selmap_t)
        os_, ow = _sel(qbt, maskt, ks, vst, kw, vwt)
        x1 = _merge(l, flat(x), g_mix, flat(oa), flat(oc), flat(os_), flat(ow), flat(gb), wp,
                    wof, won, wout, gate_e)
        x = _mlp(l, x1, g_mlp, wu, wd, norm_final[None, :], l == depth - 1).reshape(B, T, D_MODEL)
    return x
```

```python
import functools

import numpy as np
import jax
import jax.numpy as jnp
from jax import lax
from jax.experimental import pallas as pl
from jax.experimental.pallas import tpu as pltpu

F32 = jnp.float32
BF16 = jnp.bfloat16

D_MODEL = 1024
HEAD_DIM = 64
FOX_HEADS = 8
NSA_HEADS = 8
NSA_KV_HEADS = 2
NSA_GROUP = NSA_HEADS // NSA_KV_HEADS
CMP_BLOCK = 32
CMP_STRIDE = 16
CMP_HIDDEN = 4 * HEAD_DIM
SEL_BLOCK = 64
SEL_TOPK = 16
WINDOW = 512
ROPE_THETA = 500000.0
ROPE_DIM = HEAD_DIM // 4
ROPE_HALF = ROPE_DIM // 2
D_FF = 4 * D_MODEL
RMS_EPS = 1e-6
FORCE_SCORE = 1e6
NEG_BIG = -1e30
SCALE = HEAD_DIM ** -0.5
LOG2E = 1.4426950408889634

FOX_W = FOX_HEADS * HEAD_DIM
NSA_W = NSA_HEADS * HEAD_DIM
KV_W = NSA_KV_HEADS * HEAD_DIM

LANES = 128
DEN_ROWS = 16
PIECE_LANES = 32
VMEM_LIMIT_BYTES = 56 * 1024 * 1024

FOX_PAD_W = FOX_HEADS * LANES
GROUP_W = NSA_GROUP * HEAD_DIM
OFF_QA = 0
OFF_KA = OFF_QA + FOX_PAD_W
OFF_VA = OFF_KA + FOX_PAD_W
OFF_F = OFF_VA + FOX_W
OFF_QB = OFF_F + LANES
OFF_KC = OFF_QB + NSA_W
OFF_VC = OFF_KC + KV_W
OFF_KS = OFF_VC + KV_W
OFF_VS = OFF_KS + KV_W
OFF_KW = OFF_VS + KV_W
OFF_VW = OFF_KW + KV_W
OFF_GB = OFF_VW + KV_W
OFF_GATE = OFF_GB + LANES
PROJ_W = OFF_GATE + 2 * D_MODEL

AUG_Q_C = HEAD_DIM
AUG_K_C = HEAD_DIM + 3

PROJ_TM = 256
FOX_TQ = 512
FOX_TK = 512
FOX_HEADS_PER_STEP = 4
NSA_TQ = 512
SEL_TQ = 512
SEL_TK = 512
MERGE_TM = 256
MLP_TM = 512
MLP_FF_CHUNK = 512


def _cparams(sem):
    return pltpu.CompilerParams(dimension_semantics=sem, vmem_limit_bytes=VMEM_LIMIT_BYTES)


def _dot(a, b):
    return jnp.dot(a, b, preferred_element_type=F32)


def _split3(x):
    hi = x.astype(BF16)
    r = x - hi.astype(F32)
    mid = r.astype(BF16)
    lo = (r - mid.astype(F32)).astype(BF16)
    return hi, mid, lo


def _pack3(x, n_real):
    lane = lax.broadcasted_iota(jnp.int32, x.shape, 1)
    hi, mid, lo = _split3(jnp.where(lane < n_real, x, 0.0))
    return (hi.astype(F32) + pltpu.roll(mid.astype(F32), PIECE_LANES, 1)
            + pltpu.roll(lo.astype(F32), 2 * PIECE_LANES, 1)).astype(BF16)


def _lane_tile(x, reps):
    return x if reps == 1 else jnp.concatenate([x] * reps, axis=1)


def _rope(y, rc, rs1, rs2):
    n = y.shape[1]
    return y * rc + pltpu.roll(y, n - ROPE_HALF, 1) * rs1 + pltpu.roll(y, ROPE_HALF, 1) * rs2


def _rmsnorm(x, g):
    return x * lax.rsqrt(jnp.mean(x * x, axis=-1, keepdims=True) + RMS_EPS) * g


def _sigmoid(x):
    return 1.0 / (1.0 + jnp.exp(-x))


def _proj_kernel(x_ref, g_ref, w_ref, bf_ref, pq_ref, pk_ref, rc_ref, rs1_ref, rs2_ref,
                 qa_ref, ka_ref, va_ref, qb_ref, ks_ref, kw_ref, vs_ref, vw_ref, kvc_ref, gb_ref,
                 h_scr, carry_scr, *, tm):
    t = pl.program_id(1)

    @pl.when(t == 0)
    def _():
        carry_scr[...] = jnp.zeros_like(carry_scr)

    h_scr[...] = _rmsnorm(x_ref[0], g_ref[...]).astype(BF16)

    def seg(c0, n):
        return _dot(h_scr[...], w_ref[:, c0:c0 + n])

    f = seg(OFF_F, LANES) + bf_ref[...]
    lf = jnp.minimum(f, 0.0) - jnp.log1p(jnp.exp(-jnp.abs(f)))
    row = lax.broadcasted_iota(jnp.int32, (tm, tm), 0)
    col = lax.broadcasted_iota(jnp.int32, (tm, tm), 1)
    tri = jnp.where(col <= row, 1.0, 0.0).astype(BF16)
    hi, mid, lo = _split3(lf)
    c = _dot(tri, hi) + _dot(tri, mid) + _dot(tri, lo) + carry_scr[0:1, :]
    carry_scr[0:1, :] = c[tm - 1:tm, :]
    c3 = _pack3(c * LOG2E, FOX_HEADS)

    lane = lax.broadcasted_iota(jnp.int32, (tm, FOX_PAD_W), 1) & (LANES - 1)
    ones_q = jnp.where((lane >= AUG_K_C) & (lane < AUG_K_C + 3), 1.0, 0.0)
    ones_k = jnp.where((lane >= AUG_Q_C) & (lane < AUG_Q_C + 3), 1.0, 0.0)
    qa_ref[0] = (seg(OFF_QA, FOX_PAD_W) + _dot(c3, pq_ref[...]) + ones_q).T.astype(BF16)
    ka_ref[0] = (seg(OFF_KA, FOX_PAD_W) + _dot(c3, pk_ref[...]) + ones_k).astype(BF16)
    va_ref[0] = seg(OFF_VA, FOX_W).T.astype(BF16)

    rc, rs1, rs2 = rc_ref[...], rs1_ref[...], rs2_ref[...]
    reps = NSA_W // LANES
    qb_ref[0] = _rope(seg(OFF_QB, NSA_W), _lane_tile(rc, reps), _lane_tile(rs1, reps),
                      _lane_tile(rs2, reps)).T.astype(BF16)
    pos = t * tm + lax.broadcasted_iota(jnp.int32, (tm, LANES), 0)
    onehot = jnp.where((pos >> 6) == lax.broadcasted_iota(jnp.int32, (tm, LANES), 1), 1.0, 0.0)
    ks_ref[0] = jnp.concatenate([_rope(seg(OFF_KS, KV_W), rc, rs1, rs2), onehot], axis=1).astype(BF16)
    kw_ref[0] = _rope(seg(OFF_KW, KV_W), rc, rs1, rs2).astype(BF16)
    vs_ref[0] = seg(OFF_VS, KV_W).T.astype(BF16)
    vw_ref[0] = seg(OFF_VW, KV_W).T.astype(BF16)
    kvc_ref[0, 0] = seg(OFF_KC, KV_W)
    kvc_ref[1, 0] = seg(OFF_VC, KV_W)
    gb_ref[0] = seg(OFF_GB, LANES)


def _proj(layer, x, g, wp, bf, pq, pk, rc, rs1, rs2):
    B, T, _ = x.shape
    tm = PROJ_TM
    row = lambda w: pl.BlockSpec((1, tm, w), lambda b, t: (b, t, 0))
    col = lambda w: pl.BlockSpec((1, w, tm), lambda b, t: (b, 0, t))
    const = lambda shape: pl.BlockSpec(shape, lambda b, t: (0,) * len(shape))
    tab = pl.BlockSpec((tm, LANES), lambda b, t: (t, 0))
    out_shapes = [
        jax.ShapeDtypeStruct((B, FOX_PAD_W, T), BF16),
        jax.ShapeDtypeStruct((B, T, FOX_PAD_W), BF16),
        jax.ShapeDtypeStruct((B, FOX_W, T), BF16),
        jax.ShapeDtypeStruct((B, NSA_W, T), BF16),
        jax.ShapeDtypeStruct((B, T, KV_W + LANES), BF16),
        jax.ShapeDtypeStruct((B, T, KV_W), BF16),
        jax.ShapeDtypeStruct((B, KV_W, T), BF16),
        jax.ShapeDtypeStruct((B, KV_W, T), BF16),
        jax.ShapeDtypeStruct((2, B, T, KV_W), F32),
        jax.ShapeDtypeStruct((B, T, LANES), F32),
    ]
    out_specs = [
        col(FOX_PAD_W), row(FOX_PAD_W), col(FOX_W), col(NSA_W),
        row(KV_W + LANES), row(KV_W), col(KV_W), col(KV_W),
        pl.BlockSpec((2, 1, tm, KV_W), lambda b, t: (0, b, t, 0)),
        row(LANES),
    ]
    return pl.pallas_call(
        functools.partial(_proj_kernel, tm=tm),
        grid=(B, T // tm),
        in_specs=[row(D_MODEL), _layer_spec(layer, (1, D_MODEL)), _layer_spec(layer, (D_MODEL, OFF_GATE)),
                  _layer_spec(layer, (1, LANES)),
                  const((LANES, FOX_PAD_W)), const((LANES, FOX_PAD_W)), tab, tab, tab],
        out_specs=out_specs,
        out_shape=out_shapes,
        scratch_shapes=[pltpu.VMEM((tm, D_MODEL), BF16), pltpu.VMEM((8, LANES), F32)],
        compiler_params=_cparams(("arbitrary", "arbitrary")),
        name="proj",
    )(x, g, wp, bf, pq, pk, rc, rs1, rs2)


def _with_ones_rows(vt):
    first = lax.broadcasted_iota(jnp.int32, (DEN_ROWS, vt.shape[1]), 0) == 0
    return jnp.concatenate([vt, jnp.where(first, 1.0, 0.0).astype(vt.dtype)], axis=0)


def _causal_mask_t(st, i, j, tq, tk):
    kpos = j * tk + lax.broadcasted_iota(jnp.int32, (tk, tq), 0)
    qpos = i * tq + lax.broadcasted_iota(jnp.int32, (tk, tq), 1)
    return jnp.where(kpos <= qpos, st, NEG_BIG)


def _causal_flash_t(i, n_heads, tq, tk, score_fn, vt_fn, scr, window=None):
    s0_scr, s1_scr, m_scr, acc_scr = scr
    n_full = (i * tq) // tk

    def scores(j, s_scr, fn=score_fn):
        start = pl.multiple_of(j * tk, tk)
        for h in range(n_heads):
            s_scr[h] = fn(h, start)

    def softmax_pv(j, s_scr, mask=None, vt=vt_fn, m_ref=m_scr, acc_ref=acc_scr):
        start = pl.multiple_of(j * tk, tk)
        for h in range(n_heads):
            st = s_scr[h] if mask is None else mask(s_scr[h])
            m = m_ref[h]
            m_new = jnp.maximum(m, jnp.max(st, axis=0, keepdims=True))
            p = jnp.exp2(st - m_new).astype(BF16)
            acc_ref[h] = jnp.exp2(m - m_new) * acc_ref[h] + _dot(_with_ones_rows(vt(h, start)), p)
            m_ref[h] = m_new

    def normalised(acc_ref):
        return jnp.concatenate([acc_ref[h, 0:HEAD_DIM, :] / acc_ref[h, HEAD_DIM:HEAD_DIM + 1, :]
                                for h in range(n_heads)], axis=0)

    causal = lambda st: _causal_mask_t(st, i, n_full, tq, tk)
    m_scr[...] = jnp.full_like(m_scr, NEG_BIG)
    acc_scr[...] = jnp.zeros_like(acc_scr)

    odd = n_full & 1

    @pl.when(odd == 1)
    def _():
        scores(0, s1_scr)
        scores(1, s0_scr)
        softmax_pv(0, s1_scr)

    @pl.when(odd == 0)
    def _():
        scores(0, s0_scr)

    def body(jp, carry):
        t0 = odd + 2 * jp
        scores(t0 + 1, s1_scr)
        softmax_pv(t0, s0_scr)
        scores(t0 + 2, s0_scr)
        softmax_pv(t0 + 1, s1_scr)
        return carry

    lax.fori_loop(0, n_full // 2, body, 0)
    if window is None:
        softmax_pv(n_full, s0_scr, causal)
        return normalised(acc_scr), None

    assert tq == tk == WINDOW
    w_score, w_vt, wm_scr, wacc_scr = window
    wm_scr[...] = jnp.full_like(wm_scr, NEG_BIG)
    wacc_scr[...] = jnp.zeros_like(wacc_scr)
    prev = jnp.maximum(i - 1, 0)
    first_row = jnp.where(i > 0, 0, tk + tq)

    def in_window(st):
        r = lax.broadcasted_iota(jnp.int32, (tk, tq), 0)
        c = lax.broadcasted_iota(jnp.int32, (tk, tq), 1)
        return jnp.where(r - c > first_row, st, NEG_BIG)

    stats = dict(vt=w_vt, m_ref=wm_scr, acc_ref=wacc_scr)
    scores(i, s1_scr, w_score)
    softmax_pv(n_full, s0_scr, causal)
    scores(prev, s0_scr, w_score)
    softmax_pv(i, s1_scr, causal, **stats)
    softmax_pv(prev, s0_scr, in_window, **stats)
    return normalised(acc_scr), normalised(wacc_scr)


def _flash_scratch(n_heads, tq, tk):
    return [pltpu.VMEM((n_heads, tk, tq), F32), pltpu.VMEM((n_heads, tk, tq), F32),
            pltpu.VMEM((n_heads, 1, tq), F32), pltpu.VMEM((n_heads, HEAD_DIM + DEN_ROWS, tq), F32)]


def _fox_kernel(qt_ref, k_ref, vt_ref, o_ref, *scr, tq, tk, heads):
    def score_fn(h, start):
        return _dot(k_ref[0, pl.ds(start, tk), h * LANES:(h + 1) * LANES], qt_ref[0, h * LANES:(h + 1) * LANES, :])

    def vt_fn(h, start):
        return vt_ref[0, h * HEAD_DIM:(h + 1) * HEAD_DIM, pl.ds(start, tk)]

    ot, _ = _causal_flash_t(pl.program_id(2), heads, tq, tk, score_fn, vt_fn, scr)
    o_ref[0] = ot.T.astype(BF16)


def _fox(qat, ka, vat):
    B, T, _ = ka.shape
    tq, tk = min(FOX_TQ, T), min(FOX_TK, T)
    heads = FOX_HEADS_PER_STEP
    return pl.pallas_call(
        functools.partial(_fox_kernel, tq=tq, tk=tk, heads=heads),
        grid=(B, FOX_HEADS // heads, T // tq),
        in_specs=[pl.BlockSpec((1, heads * LANES, tq), lambda b, h, i: (b, h, i)),
                  pl.BlockSpec((1, T, heads * LANES), lambda b, h, i: (b, 0, h)),
                  pl.BlockSpec((1, heads * HEAD_DIM, T), lambda b, h, i: (b, h, 0))],
        out_specs=pl.BlockSpec((1, tq, heads * HEAD_DIM), lambda b, h, i: (b, i, h)),
        out_shape=jax.ShapeDtypeStruct((B, T, FOX_W), BF16),
        scratch_shapes=_flash_scratch(heads, tq, tk),
        compiler_params=_cparams(("arbitrary", "arbitrary", "arbitrary")),
        name="fox",
    )(qat, ka, vat)


def _cmp_kernel(r_ref, w1_ref, pos_ref, b1_ref, w2_ref, b2_ref, rc_ref, rs1_ref, rs2_ref, o_ref, ot_ref):
    nr = o_ref.shape[0]
    half = CMP_BLOCK // 2
    a = bm = None
    for l in range(half):
        x = r_ref[pl.ds(l, nr, stride=CMP_STRIDE), :]
        ta = _dot((x + pos_ref[l:l + 1, :]).astype(BF16), w1_ref[0, l * KV_W:(l + 1) * KV_W, :])
        tb = _dot((x + pos_ref[half + l:half + l + 1, :]).astype(BF16), w1_ref[1, l * KV_W:(l + 1) * KV_W, :])
        a = ta if a is None else a + ta
        bm = tb if bm is None else bm + tb
    hdn = a + pltpu.roll(bm, nr - 1, 0) + b1_ref[...]
    act = 0.5 * hdn * (1.0 + jnp.tanh(0.7978845608028654 * (hdn + 0.044715 * (hdn * hdn * hdn))))
    y = _rope(_dot(act.astype(BF16), w2_ref[...]) + b2_ref[...], rc_ref[...], rs1_ref[...], rs2_ref[...])
    o_ref[...] = y.astype(BF16)
    ot_ref[...] = y.T.astype(BF16)


def _cmp(layer, kvc, w1, pos, b1, w2, b2, rc, rs1, rs2):
    _, B, T, _ = kvc.shape
    nr = T // CMP_STRIDE
    rw = CMP_STRIDE * KV_W
    per_kv = lambda shape: pl.BlockSpec((None, None) + shape, lambda s, b: (layer, s) + (0,) * len(shape))
    tab = pl.BlockSpec((None, nr, LANES), lambda s, b: (s, 0, 0))
    return pl.pallas_call(
        _cmp_kernel,
        grid=(2, B),
        in_specs=[pl.BlockSpec((None, None, T, KV_W), lambda s, b: (s, b, 0, 0)),
                  per_kv((2, rw, 2 * CMP_HIDDEN)), per_kv((CMP_BLOCK, KV_W)), per_kv((1, 2 * CMP_HIDDEN)),
                  per_kv((2 * CMP_HIDDEN, KV_W)), per_kv((1, KV_W)), tab, tab, tab],
        out_specs=[pl.BlockSpec((None, None, nr, KV_W), lambda s, b: (s, b, 0, 0)),
                   pl.BlockSpec((None, None, KV_W, nr), lambda s, b: (s, b, 0, 0))],
        out_shape=[jax.ShapeDtypeStruct((2, B, nr, KV_W), BF16), jax.ShapeDtypeStruct((2, B, KV_W, nr), BF16)],
        compiler_params=_cparams(("arbitrary", "arbitrary")),
        name="cmp",
    )(kvc, w1, pos, b1, w2, b2, rc, rs1, rs2)


def _group_query_weights(qt, g):
    zero = jnp.zeros((HEAD_DIM, qt.shape[1]), qt.dtype)
    out = []
    for r in range(NSA_GROUP):
        q = qt[r * HEAD_DIM:(r + 1) * HEAD_DIM, :]
        out.append(jnp.concatenate([jnp.where(g == 0, q, zero), jnp.where(g == 1, q, zero)], axis=0))
    return out


def _cmpsel_kernel(qt_ref, kc_ref, vct_ref, mapt_ref, oc_ref, mt_ref, *, tq, n_sel, top):
    g = pl.program_id(1)
    i = pl.program_id(2)
    nr = kc_ref.shape[2]
    kc = kc_ref[0, 0]
    vct = vct_ref[0, 0]
    cend = lax.broadcasted_iota(jnp.int32, (nr, tq), 0) * CMP_STRIDE + (CMP_BLOCK - 1)
    qpos = i * tq + lax.broadcasted_iota(jnp.int32, (nr, tq), 1)
    valid = cend <= qpos
    sts = [_dot(kc, w) for w in _group_query_weights(qt_ref[0], g)]
    vct_ones = _with_ones_rows(vct)
    imp = jnp.zeros((LANES, tq), F32)
    outs = []
    for st in sts:
        st = jnp.where(valid, st, NEG_BIG)
        e = jnp.where(valid, jnp.exp2(st - jnp.max(st, axis=0, keepdims=True)), 0.0).astype(BF16)
        oe = _dot(vct_ones, e)
        den = oe[HEAD_DIM:HEAD_DIM + 1, :]
        inv = jnp.where(den > 0.0, 1.0 / jnp.where(den > 0.0, den, 1.0), 0.0)
        outs.append(oe[0:HEAD_DIM, :] * inv)
        imp = imp + _dot(mapt_ref[...], e) * inv
    oc_ref[0] = jnp.concatenate(outs, axis=0).T

    sid = lax.broadcasted_iota(jnp.int32, (LANES, tq), 0)
    qp = i * tq + lax.broadcasted_iota(jnp.int32, (LANES, tq), 1)
    score0 = jnp.where((sid * SEL_BLOCK > qp) | (sid >= n_sel), -3.0,
                       jnp.where(sid == (qp >> 6), FORCE_SCORE, jnp.where(sid == 0, 2.0 * FORCE_SCORE, imp)))

    score = score0
    for _ in range(top):
        score = jnp.where(score == jnp.max(score, axis=0, keepdims=True), -2.0, score)
    picked = score == -2.0
    mt_ref[0, 0] = jnp.where(picked, 0.0, NEG_BIG).astype(BF16)
    n_picked = jnp.sum(jnp.where(picked, 1.0, 0.0), axis=0, keepdims=True)

    @pl.when(jnp.max(n_picked) > top)
    def _():
        sidf = sid.astype(F32)
        score = score0
        maskt = jnp.full((LANES, tq), NEG_BIG, F32)
        for _ in range(top):
            mx = jnp.max(score, axis=0, keepdims=True)
            idx = jnp.min(jnp.where(score == mx, sidf, float(LANES)), axis=0, keepdims=True)
            pick = (sidf == idx) & (mx > -3.0)
            maskt = jnp.where(pick, 0.0, maskt)
            score = jnp.where(pick, -2.0, score)
        mt_ref[0, 0] = maskt.astype(BF16)


def _cmpsel(qbt, kc, vct, selmap_t):
    B, _, T = qbt.shape
    nr = kc.shape[2]
    tq = NSA_TQ
    n_sel = T // SEL_BLOCK
    return pl.pallas_call(
        functools.partial(_cmpsel_kernel, tq=tq, n_sel=n_sel, top=min(SEL_TOPK, n_sel)),
        grid=(B, NSA_KV_HEADS, T // tq),
        in_specs=[pl.BlockSpec((1, GROUP_W, tq), lambda b, g, i: (b, g, i)),
                  pl.BlockSpec((1, 1, nr, KV_W), lambda b, g, i: (0, b, 0, 0)),
                  pl.BlockSpec((1, 1, HEAD_DIM, nr), lambda b, g, i: (1, b, g, 0)),
                  pl.BlockSpec((LANES, nr), lambda b, g, i: (0, 0))],
        out_specs=[pl.BlockSpec((1, tq, GROUP_W), lambda b, g, i: (b, i, g)),
                   pl.BlockSpec((1, 1, LANES, tq), lambda b, g, i: (b, g, 0, i))],
        out_shape=[jax.ShapeDtypeStruct((B, T, NSA_W), F32),
                   jax.ShapeDtypeStruct((B, NSA_KV_HEADS, LANES, T), BF16)],
        compiler_params=_cparams(("arbitrary", "arbitrary", "arbitrary")),
        name="cmpsel",
    )(qbt, kc, vct, selmap_t)


def _sel_kernel(qt_ref, mt_ref, k_ref, vt_ref, kw_ref, vwt_ref, os_ref, ow_ref, w_scr, wm_scr, wacc_scr, *scr,
                tq, tk):
    g = pl.program_id(1)
    for r, w in enumerate(_group_query_weights(qt_ref[0], g)):
        w_scr[r] = jnp.concatenate([w, mt_ref[0, 0]], axis=0)

    def score_fn(r, start):
        return _dot(k_ref[0, pl.ds(start, tk), :], w_scr[r])

    def vt_fn(r, start):
        return vt_ref[0, :, pl.ds(start, tk)]

    def w_score_fn(r, start):
        return _dot(kw_ref[0, pl.ds(start, tk), :], w_scr[r, 0:KV_W, :])

    def w_vt_fn(r, start):
        return vwt_ref[0, :, pl.ds(start, tk)]

    ost, owt = _causal_flash_t(pl.program_id(2), NSA_GROUP, tq, tk, score_fn, vt_fn, scr,
                               window=(w_score_fn, w_vt_fn, wm_scr, wacc_scr))
    os_ref[0] = ost.T
    ow_ref[0] = owt.T


def _sel(qbt, maskt, ks, vst, kw, vwt):
    B, _, T = qbt.shape
    tq, tk = min(SEL_TQ, T), min(SEL_TK, T)
    kaug = KV_W + LANES
    out = pl.BlockSpec((1, tq, GROUP_W), lambda b, g, i: (b, i, g))
    return pl.pallas_call(
        functools.partial(_sel_kernel, tq=tq, tk=tk),
        grid=(B, NSA_KV_HEADS, T // tq),
        in_specs=[pl.BlockSpec((1, GROUP_W, tq), lambda b, g, i: (b, g, i)),
                  pl.BlockSpec((1, 1, LANES, tq), lambda b, g, i: (b, g, 0, i)),
                  pl.BlockSpec((1, T, kaug), lambda b, g, i: (b, 0, 0)),
                  pl.BlockSpec((1, HEAD_DIM, T), lambda b, g, i: (b, g, 0)),
                  pl.BlockSpec((1, T, KV_W), lambda b, g, i: (b, 0, 0)),
                  pl.BlockSpec((1, HEAD_DIM, T), lambda b, g, i: (b, g, 0))],
        out_specs=[out, out],
        out_shape=[jax.ShapeDtypeStruct((B, T, NSA_W), F32)] * 2,
        scratch_shapes=[pltpu.VMEM((NSA_GROUP, kaug, tq), BF16)] + _flash_scratch(NSA_GROUP, tq, tk)[2:]
        + _flash_scratch(NSA_GROUP, tq, tk),
        compiler_params=_cparams(("arbitrary", "arbitrary", "arbitrary")),
        name="sel",
    )(qbt, maskt, ks, vst, kw, vwt)


def _merge_kernel(x_ref, g_ref, oa_ref, oc_ref, os_ref, ow_ref, gb_ref, wg_ref, wof_ref, won_ref, wout_ref, e_ref,
                  o_ref):
    x = x_ref[...]
    h = _rmsnorm(x, g_ref[...]).astype(BF16)
    ya = _dot(oa_ref[...], wof_ref[...])
    sg = _pack3(_sigmoid(gb_ref[...]), 3 * NSA_HEADS)
    on = (_dot(sg, e_ref[0]) * oc_ref[...] + _dot(sg, e_ref[1]) * os_ref[...]
          + _dot(sg, e_ref[2]) * ow_ref[...])
    yb = _dot(on.astype(BF16), won_ref[...])
    mixed = (_sigmoid(_dot(h, wg_ref[:, 0:D_MODEL])) * ya
             + _sigmoid(_dot(h, wg_ref[:, D_MODEL:2 * D_MODEL])) * yb)
    o_ref[...] = x + _dot(mixed.astype(BF16), wout_ref[...])


def _layer_spec(layer, shape):
    return pl.BlockSpec((None,) + shape, lambda *_: (layer,) + (0,) * len(shape))


def _merge(layer, x, g, oa, oc, os_, ow, gb, wp, wof, won, wout, e):
    n = x.shape[0]
    tm = MERGE_TM
    row = lambda w: pl.BlockSpec((tm, w), lambda t: (t, 0))
    const = lambda shape: pl.BlockSpec(shape, lambda t: (0,) * len(shape))
    assert OFF_GATE % (2 * D_MODEL) == 0
    gate_w = pl.BlockSpec((None, D_MODEL, 2 * D_MODEL), lambda t: (layer, 0, OFF_GATE // (2 * D_MODEL)))
    return pl.pallas_call(
        _merge_kernel,
        grid=(n // tm,),
        in_specs=[row(D_MODEL), _layer_spec(layer, (1, D_MODEL)), row(FOX_W), row(NSA_W), row(NSA_W), row(NSA_W),
                  row(LANES), gate_w,
                  _layer_spec(layer, (FOX_W, D_MODEL)), _layer_spec(layer, (NSA_W, D_MODEL)),
                  _layer_spec(layer, (D_MODEL, D_MODEL)), const((3, LANES, NSA_W))],
        out_specs=row(D_MODEL),
        out_shape=jax.ShapeDtypeStruct((n, D_MODEL), F32),
        compiler_params=_cparams(("arbitrary",)),
        name="merge",
    )(x, g, oa, oc, os_, ow, gb, wp, wof, won, wout, e)


def _mlp_kernel(x_ref, g_ref, wu_ref, wd_ref, gf_ref, o_ref, h_scr, *, final_norm):
    x = x_ref[...]
    h_scr[...] = _rmsnorm(x, g_ref[...]).astype(BF16)
    acc = None
    for c0 in range(0, D_FF, MLP_FF_CHUNK):
        u = jnp.maximum(_dot(h_scr[...], wu_ref[:, c0:c0 + MLP_FF_CHUNK]), 0.0)
        part = _dot((u * u).astype(BF16), wd_ref[c0:c0 + MLP_FF_CHUNK, :])
        acc = part if acc is None else acc + part
    y = x + acc
    o_ref[...] = _rmsnorm(y, gf_ref[...]) if final_norm else y


def _mlp(layer, x, g, wu, wd, gf, final_norm):
    n = x.shape[0]
    tm = MLP_TM
    row = pl.BlockSpec((tm, D_MODEL), lambda t: (t, 0))
    const = lambda shape: pl.BlockSpec(shape, lambda t: (0,) * len(shape))
    return pl.pallas_call(
        functools.partial(_mlp_kernel, final_norm=final_norm),
        grid=(n // tm,),
        in_specs=[row, _layer_spec(layer, (1, D_MODEL)), _layer_spec(layer, (D_MODEL, D_FF)),
                  _layer_spec(layer, (D_FF, D_MODEL)), const((1, D_MODEL))],
        out_specs=row,
        out_shape=jax.ShapeDtypeStruct((n, D_MODEL), F32),
        scratch_shapes=[pltpu.VMEM((tm, D_MODEL), BF16)],
        compiler_params=_cparams(("arbitrary",)),
        name="mlp",
    )(x, g, wu, wd, gf)


def _rope_tables(pos):
    inv = ROPE_THETA ** (-jnp.arange(ROPE_HALF, dtype=F32) / ROPE_HALF)
    ang = pos.astype(F32)[:, None] * inv[None, :]
    cos, sin = jnp.cos(ang), jnp.sin(ang)
    n = pos.shape[0]
    rest = HEAD_DIM - ROPE_DIM
    rc = jnp.concatenate([cos, cos, jnp.ones((n, rest), F32)], axis=1)
    rs1 = jnp.concatenate([-sin, jnp.zeros((n, HEAD_DIM - ROPE_HALF), F32)], axis=1)
    rs2 = jnp.concatenate([jnp.zeros((n, ROPE_HALF), F32), sin, jnp.zeros((n, rest), F32)], axis=1)
    two = lambda t: jnp.concatenate([t, t], axis=1)
    return two(rc), two(rs1), two(rs2)


def _pad_heads(w, n_heads):
    lead = w.shape[:-1]
    w = w.reshape(lead + (n_heads, HEAD_DIM))
    return jnp.concatenate([w, jnp.zeros_like(w)], axis=-1).reshape(lead + (n_heads * LANES,))


def _pad_cols(w, width):
    return jnp.concatenate([w, jnp.zeros(w.shape[:-1] + (width - w.shape[-1],), w.dtype)], axis=-1)


def _proj_weight(w_in):
    o = np.cumsum([0, FOX_W, FOX_W, FOX_W, FOX_HEADS, NSA_W, 6 * KV_W, 3 * NSA_HEADS, D_MODEL, D_MODEL])
    cols = lambda j: w_in[..., o[j]:o[j + 1]]
    qscale = SCALE * LOG2E
    parts = [
        _pad_heads(cols(0) * qscale, FOX_HEADS), _pad_heads(cols(1), FOX_HEADS), cols(2),
        _pad_cols(cols(3), LANES),
        cols(4) * qscale,
        cols(5),
        _pad_cols(cols(6), LANES),
        cols(7), cols(8),
    ]
    return jnp.concatenate(parts, axis=-1).astype(BF16)


def _aug_placement():
    pq = np.zeros((LANES, FOX_PAD_W), np.float32)
    pk = np.zeros((LANES, FOX_PAD_W), np.float32)
    for h in range(FOX_HEADS):
        for piece in range(3):
            pq[piece * PIECE_LANES + h, h * LANES + AUG_Q_C + piece] = 1.0
            pk[piece * PIECE_LANES + h, h * LANES + AUG_K_C + piece] = -1.0
    return jnp.asarray(pq, BF16), jnp.asarray(pk, BF16)


def _gate_placement():
    e = np.zeros((3, LANES, NSA_W), np.float32)
    for j in range(3):
        for h in range(NSA_HEADS):
            for piece in range(3):
                e[j, piece * PIECE_LANES + 3 * h + j, h * HEAD_DIM:(h + 1) * HEAD_DIM] = 1.0
    return jnp.asarray(e, BF16)


def _sel_map_t(T, nr):
    nc = (T - CMP_BLOCK) // CMP_STRIDE + 1
    n_sel = T // SEL_BLOCK
    cs = np.arange(nc) * CMP_STRIDE
    ce = cs + CMP_BLOCK
    ss = np.arange(n_sel) * SEL_BLOCK
    se = ss + SEL_BLOCK
    ov = np.clip(np.minimum(ce[None, :], se[:, None]) - np.maximum(cs[None, :], ss[:, None]), 0, None)
    m = np.zeros((LANES, nr), np.float32)
    m[:n_sel, :nc] = ov / CMP_BLOCK
    return jnp.asarray(m, BF16)


def _cmp_weights(cmp_pos, cmp_w1, cmp_b1, cmp_w2, cmp_b2):
    d, half, g2 = cmp_w1.shape[0], CMP_BLOCK // 2, NSA_KV_HEADS

    def group_diag(w, axis):
        z = jnp.zeros_like(w)
        return jnp.stack([jnp.concatenate([w, z], axis=-1), jnp.concatenate([z, w], axis=-1)], axis=axis)

    w1 = group_diag(cmp_w1.astype(BF16).reshape(d, 2, 2, half, HEAD_DIM, CMP_HIDDEN), 4)
    w1 = w1.reshape(d, 2, 2, half * KV_W, g2 * CMP_HIDDEN)
    pos = jnp.concatenate([cmp_pos] * g2, axis=-1)
    b1 = jnp.concatenate([cmp_b1] * g2, axis=-1)[:, :, None, :]
    w2 = group_diag(cmp_w2.astype(BF16), 2).reshape(d, 2, g2 * CMP_HIDDEN, KV_W)
    b2 = jnp.concatenate([cmp_b2] * g2, axis=-1)[:, :, None, :]
    return w1, pos, b1, w2, b2


def kernel(x, norm_mix, w_in, b_forget, cmp_pos, cmp_w1, cmp_b1, cmp_w2, cmp_b2, w_o_fox, w_o_nsa, w_out,
           norm_mlp, w_up, w_down, norm_final):
    B, T, _ = x.shape
    depth = w_in.shape[0]
    nr = T // CMP_STRIDE
    n = B * T
    flat = lambda a: a.reshape(n, a.shape[-1])

    pq, pk = _aug_placement()
    gate_e = _gate_placement()
    selmap_t = _sel_map_t(T, nr)
    rope = _rope_tables(jnp.arange(T))
    crope = [jnp.stack([t, jnp.full_like(t, fill)]) for t, fill in
             zip(_rope_tables(jnp.arange(nr) * CMP_STRIDE + CMP_BLOCK - 1), (1.0, 0.0, 0.0))]
    wp = _proj_weight(w_in)
    bf = _pad_cols(b_forget, LANES)[:, None, :]
    cmp_w = _cmp_weights(cmp_pos, cmp_w1, cmp_b1, cmp_w2, cmp_b2)
    wof, won, wout = w_o_fox.astype(BF16), w_o_nsa.astype(BF16), w_out.astype(BF16)
    wu, wd = w_up.astype(BF16), w_down.astype(BF16)
    g_mix, g_mlp = norm_mix[:, None, :], norm_mlp[:, None, :]

    for l in range(depth):
        qat, ka, vat, qbt, ks, kw, vst, vwt, kvc, gb = _proj(l, x, g_mix, wp, bf, pq, pk, *rope)
        oa = _fox(qat, ka, vat)
        cmp_rows, cmp_cols = _cmp(l, kvc, *cmp_w, *crope)
        oc, maskt = _cmpsel(qbt, cmp_rows, cmp_cols, selmap_t)
        os_, ow = _sel(qbt, maskt, ks, vst, kw, vwt)
        x1 = _merge(l, flat(x), g_mix, flat(oa), flat(oc), flat(os_), flat(ow), flat(gb), wp,
                    wof, won, wout, gate_e)
        x = _mlp(l, x1, g_mlp, wu, wd, norm_final[None, :], l == depth - 1).reshape(B, T, D_MODEL)
    return x
```

```python
import functools

import numpy as np
import jax
import jax.numpy as jnp
from jax import lax
from jax.experimental import pallas as pl
from jax.experimental.pallas import tpu as pltpu

F32 = jnp.float32
BF16 = jnp.bfloat16

D_MODEL = 1024
HEAD_DIM = 64
FOX_HEADS = 8
NSA_HEADS = 8
NSA_KV_HEADS = 2
NSA_GROUP = NSA_HEADS // NSA_KV_HEADS
CMP_BLOCK = 32
CMP_STRIDE = 16
CMP_HIDDEN = 4 * HEAD_DIM
SEL_BLOCK = 64
SEL_TOPK = 16
WINDOW = 512
ROPE_THETA = 500000.0
ROPE_DIM = HEAD_DIM // 4
ROPE_HALF = ROPE_DIM // 2
D_FF = 4 * D_MODEL
RMS_EPS = 1e-6
FORCE_SCORE = 1e6
NEG_BIG = -1e30
SCALE = HEAD_DIM ** -0.5
LOG2E = 1.4426950408889634

FOX_W = FOX_HEADS * HEAD_DIM
NSA_W = NSA_HEADS * HEAD_DIM
KV_W = NSA_KV_HEADS * HEAD_DIM

LANES = 128
DEN_ROWS = 16
PIECE_LANES = 32
VMEM_LIMIT_BYTES = 56 * 1024 * 1024

FOX_PAD_W = FOX_HEADS * LANES
GROUP_W = NSA_GROUP * HEAD_DIM
OFF_QA = 0
OFF_KA = OFF_QA + FOX_PAD_W
OFF_VA = OFF_KA + FOX_PAD_W
OFF_F = OFF_VA + FOX_W
OFF_QB = OFF_F + LANES
OFF_KC = OFF_QB + NSA_W
OFF_VC = OFF_KC + KV_W
OFF_KS = OFF_VC + KV_W
OFF_VS = OFF_KS + KV_W
OFF_KW = OFF_VS + KV_W
OFF_VW = OFF_KW + KV_W
OFF_GB = OFF_VW + KV_W
OFF_GATE = OFF_GB + LANES
PROJ_W = OFF_GATE + 2 * D_MODEL

AUG_Q_C = HEAD_DIM
AUG_K_C = HEAD_DIM + 3

PROJ_TM = 256
FOX_TQ = 512
FOX_TK = 512
FOX_HEADS_PER_STEP = 4
NSA_TQ = 512
SEL_TQ = 512
SEL_TK = 512
MERGE_TM = 256
MLP_TM = 512
MLP_FF_CHUNK = 512


def _cparams(sem):
    return pltpu.CompilerParams(dimension_semantics=sem, vmem_limit_bytes=VMEM_LIMIT_BYTES)


def _dot(a, b):
    return jnp.dot(a, b, preferred_element_type=F32)


def _split3(x):
    hi = x.astype(BF16)
    r = x - hi.astype(F32)
    mid = r.astype(BF16)
    lo = (r - mid.astype(F32)).astype(BF16)
    return hi, mid, lo


def _pack3(x, n_real):
    lane = lax.broadcasted_iota(jnp.int32, x.shape, 1)
    hi, mid, lo = _split3(jnp.where(lane < n_real, x, 0.0))
    return (hi.astype(F32) + pltpu.roll(mid.astype(F32), PIECE_LANES, 1)
            + pltpu.roll(lo.astype(F32), 2 * PIECE_LANES, 1)).astype(BF16)


def _lane_tile(x, reps):
    return x if reps == 1 else jnp.concatenate([x] * reps, axis=1)


def _rope(y, rc, rs1, rs2):
    n = y.shape[1]
    return y * rc + pltpu.roll(y, n - ROPE_HALF, 1) * rs1 + pltpu.roll(y, ROPE_HALF, 1) * rs2


def _rmsnorm(x, g):
    return x * lax.rsqrt(jnp.mean(x * x, axis=-1, keepdims=True) + RMS_EPS) * g


def _sigmoid(x):
    return 1.0 / (1.0 + jnp.exp(-x))


def _proj_kernel(x_ref, g_ref, w_ref, bf_ref, pq_ref, pk_ref, rc_ref, rs1_ref, rs2_ref,
                 qa_ref, ka_ref, va_ref, qb_ref, ks_ref, kw_ref, vs_ref, vw_ref, kvc_ref, gb_ref,
                 h_scr, carry_scr, *, tm):
    t = pl.program_id(1)

    @pl.when(t == 0)
    def _():
        carry_scr[...] = jnp.zeros_like(carry_scr)

    h_scr[...] = _rmsnorm(x_ref[0], g_ref[...]).astype(BF16)

    def seg(c0, n):
        return _dot(h_scr[...], w_ref[:, c0:c0 + n])

    f = seg(OFF_F, LANES) + bf_ref[...]
    lf = jnp.minimum(f, 0.0) - jnp.log1p(jnp.exp(-jnp.abs(f)))
    row = lax.broadcasted_iota(jnp.int32, (tm, tm), 0)
    col = lax.broadcasted_iota(jnp.int32, (tm, tm), 1)
    tri = jnp.where(col <= row, 1.0, 0.0).astype(BF16)
    hi, mid, lo = _split3(lf)
    c = _dot(tri, hi) + _dot(tri, mid) + _dot(tri, lo) + carry_scr[0:1, :]
    carry_scr[0:1, :] = c[tm - 1:tm, :]
    c3 = _pack3(c * LOG2E, FOX_HEADS)

    lane = lax.broadcasted_iota(jnp.int32, (tm, FOX_PAD_W), 1) & (LANES - 1)
    ones_q = jnp.where((lane >= AUG_K_C) & (lane < AUG_K_C + 3), 1.0, 0.0)
    ones_k = jnp.where((lane >= AUG_Q_C) & (lane < AUG_Q_C + 3), 1.0, 0.0)
    qa_ref[0] = (seg(OFF_QA, FOX_PAD_W) + _dot(c3, pq_ref[...]) + ones_q).T.astype(BF16)
    ka_ref[0] = (seg(OFF_KA, FOX_PAD_W) + _dot(c3, pk_ref[...]) + ones_k).astype(BF16)
    va_ref[0] = seg(OFF_VA, FOX_W).T.astype(BF16)

    rc, rs1, rs2 = rc_ref[...], rs1_ref[...], rs2_ref[...]
    reps = NSA_W // LANES
    qb_ref[0] = _rope(seg(OFF_QB, NSA_W), _lane_tile(rc, reps), _lane_tile(rs1, reps),
                      _lane_tile(rs2, reps)).T.astype(BF16)
    pos = t * tm + lax.broadcasted_iota(jnp.int32, (tm, LANES), 0)
    onehot = jnp.where((pos >> 6) == lax.broadcasted_iota(jnp.int32, (tm, LANES), 1), 1.0, 0.0)
    ks_ref[0] = jnp.concatenate([_rope(seg(OFF_KS, KV_W), rc, rs1, rs2), onehot], axis=1).astype(BF16)
    kw_ref[0] = _rope(seg(OFF_KW, KV_W), rc, rs1, rs2).astype(BF16)
    vs_ref[0] = seg(OFF_VS, KV_W).T.astype(BF16)
    vw_ref[0] = seg(OFF_VW, KV_W).T.astype(BF16)
    kvc_ref[0, 0] = seg(OFF_KC, KV_W)
    kvc_ref[1, 0] = seg(OFF_VC, KV_W)
    gb_ref[0] = seg(OFF_GB, LANES)


def _proj(layer, x, g, wp, bf, pq, pk, rc, rs1, rs2):
    B, T, _ = x.shape
    tm = PROJ_TM
    row = lambda w: pl.BlockSpec((1, tm, w), lambda b, t: (b, t, 0))
    col = lambda w: pl.BlockSpec((1, w, tm), lambda b, t: (b, 0, t))
    const = lambda shape: pl.BlockSpec(shape, lambda b, t: (0,) * len(shape))
    tab = pl.BlockSpec((tm, LANES), lambda b, t: (t, 0))
    out_shapes = [
        jax.ShapeDtypeStruct((B, FOX_PAD_W, T), BF16),
        jax.ShapeDtypeStruct((B, T, FOX_PAD_W), BF16),
        jax.ShapeDtypeStruct((B, FOX_W, T), BF16),
        jax.ShapeDtypeStruct((B, NSA_W, T), BF16),
        jax.ShapeDtypeStruct((B, T, KV_W + LANES), BF16),
        jax.ShapeDtypeStruct((B, T, KV_W), BF16),
        jax.ShapeDtypeStruct((B, KV_W, T), BF16),
        jax.ShapeDtypeStruct((B, KV_W, T), BF16),
        jax.ShapeDtypeStruct((2, B, T, KV_W), F32),
        jax.ShapeDtypeStruct((B, T, LANES), F32),
    ]
    out_specs = [
        col(FOX_PAD_W), row(FOX_PAD_W), col(FOX_W), col(NSA_W),
        row(KV_W + LANES), row(KV_W), col(KV_W), col(KV_W),
        pl.BlockSpec((2, 1, tm, KV_W), lambda b, t: (0, b, t, 0)),
        row(LANES),
    ]
    return pl.pallas_call(
        functools.partial(_proj_kernel, tm=tm),
        grid=(B, T // tm),
        in_specs=[row(D_MODEL), _layer_spec(layer, (1, D_MODEL)), _layer_spec(layer, (D_MODEL, OFF_GATE)),
                  _layer_spec(layer, (1, LANES)),
                  const((LANES, FOX_PAD_W)), const((LANES, FOX_PAD_W)), tab, tab, tab],
        out_specs=out_specs,
        out_shape=out_shapes,
        scratch_shapes=[pltpu.VMEM((tm, D_MODEL), BF16), pltpu.VMEM((8, LANES), F32)],
        compiler_params=_cparams(("arbitrary", "arbitrary")),
        name="proj",
    )(x, g, wp, bf, pq, pk, rc, rs1, rs2)


def _with_ones_rows(vt):
    first = lax.broadcasted_iota(jnp.int32, (DEN_ROWS, vt.shape[1]), 0) == 0
    return jnp.concatenate([vt, jnp.where(first, 1.0, 0.0).astype(vt.dtype)], axis=0)


def _causal_mask_t(st, i, j, tq, tk):
    kpos = j * tk + lax.broadcasted_iota(jnp.int32, (tk, tq), 0)
    qpos = i * tq + lax.broadcasted_iota(jnp.int32, (tk, tq), 1)
    return jnp.where(kpos <= qpos, st, NEG_BIG)


def _causal_flash_t(i, n_heads, tq, tk, score_fn, vt_fn, scr, window=None):
    s0_scr, s1_scr, m_scr, acc_scr = scr
    n_full = (i * tq) // tk

    def scores(j, s_scr, fn=score_fn):
        start = pl.multiple_of(j * tk, tk)
        for h in range(n_heads):
            s_scr[h] = fn(h, start)

    def softmax_pv(j, s_scr, mask=None, vt=vt_fn, m_ref=m_scr, acc_ref=acc_scr):
        start = pl.multiple_of(j * tk, tk)
        for h in range(n_heads):
            st = s_scr[h] if mask is None else mask(s_scr[h])
            m = m_ref[h]
            m_new = jnp.maximum(m, jnp.max(st, axis=0, keepdims=True))
            p = jnp.exp2(st - m_new).astype(BF16)
            acc_ref[h] = jnp.exp2(m - m_new) * acc_ref[h] + _dot(_with_ones_rows(vt(h, start)), p)
            m_ref[h] = m_new

    def normalised(acc_ref):
        return jnp.concatenate([acc_ref[h, 0:HEAD_DIM, :] / acc_ref[h, HEAD_DIM:HEAD_DIM + 1, :]
                                for h in range(n_heads)], axis=0)

    causal = lambda st: _causal_mask_t(st, i, n_full, tq, tk)
    m_scr[...] = jnp.full_like(m_scr, NEG_BIG)
    acc_scr[...] = jnp.zeros_like(acc_scr)

    odd = n_full & 1

    @pl.when(odd == 1)
    def _():
        scores(0, s1_scr)
        scores(1, s0_scr)
        softmax_pv(0, s1_scr)

    @pl.when(odd == 0)
    def _():
        scores(0, s0_scr)

    def pair(t0):
        scores(t0 + 1, s1_scr)
        softmax_pv(t0, s0_scr)
        scores(t0 + 2, s0_scr)
        softmax_pv(t0 + 1, s1_scr)

    n_pairs = n_full // 2
    odd_pair = n_pairs & 1

    @pl.when(odd_pair == 1)
    def _():
        pair(odd)

    def body(jq, carry):
        t0 = odd + 2 * odd_pair + 4 * jq
        pair(t0)
        pair(t0 + 2)
        return carry

    lax.fori_loop(0, n_pairs // 2, body, 0)
    if window is None:
        softmax_pv(n_full, s0_scr, causal)
        return normalised(acc_scr), None

    assert tq == tk == WINDOW
    w_score, w_vt = window
    prev = jnp.maximum(i - 1, 0)
    first_row = jnp.where(i > 0, 0, tk + tq)
    scores(i, s1_scr, w_score)
    softmax_pv(n_full, s0_scr, causal)
    scores(prev, s0_scr, w_score)
    rel = (lax.broadcasted_iota(jnp.int32, (tk, tq), 0) - lax.broadcasted_iota(jnp.int32, (tk, tq), 1))
    cur, start, start_prev = rel <= 0, pl.multiple_of(i * tk, tk), pl.multiple_of(prev * tk, tk)
    outs = []
    for h in range(n_heads):
        st = jnp.where(cur, s1_scr[h], jnp.where(rel > first_row, s0_scr[h], NEG_BIG))
        e = jnp.exp2(st - jnp.max(st, axis=0, keepdims=True))
        oe = (_dot(_with_ones_rows(w_vt(h, start)), jnp.where(cur, e, 0.0).astype(BF16))
              + _dot(_with_ones_rows(w_vt(h, start_prev)), jnp.where(cur, 0.0, e).astype(BF16)))
        outs.append(oe[0:HEAD_DIM, :] / oe[HEAD_DIM:HEAD_DIM + 1, :])
    return normalised(acc_scr), jnp.concatenate(outs, axis=0)


def _flash_scratch(n_heads, tq, tk):
    return [pltpu.VMEM((n_heads, tk, tq), F32), pltpu.VMEM((n_heads, tk, tq), F32),
            pltpu.VMEM((n_heads, 1, tq), F32), pltpu.VMEM((n_heads, HEAD_DIM + DEN_ROWS, tq), F32)]


def _fox_kernel(qt_ref, k_ref, vt_ref, o_ref, *scr, tq, tk, heads):
    def score_fn(h, start):
        return _dot(k_ref[0, pl.ds(start, tk), h * LANES:(h + 1) * LANES], qt_ref[0, h * LANES:(h + 1) * LANES, :])

    def vt_fn(h, start):
        return vt_ref[0, h * HEAD_DIM:(h + 1) * HEAD_DIM, pl.ds(start, tk)]

    ot, _ = _causal_flash_t(pl.program_id(2), heads, tq, tk, score_fn, vt_fn, scr)
    o_ref[0] = ot.T.astype(BF16)


def _fox(qat, ka, vat):
    B, T, _ = ka.shape
    tq, tk = min(FOX_TQ, T), min(FOX_TK, T)
    heads = FOX_HEADS_PER_STEP
    return pl.pallas_call(
        functools.partial(_fox_kernel, tq=tq, tk=tk, heads=heads),
        grid=(B, FOX_HEADS // heads, T // tq),
        in_specs=[pl.BlockSpec((1, heads * LANES, tq), lambda b, h, i: (b, h, i)),
                  pl.BlockSpec((1, T, heads * LANES), lambda b, h, i: (b, 0, h)),
                  pl.BlockSpec((1, heads * HEAD_DIM, T), lambda b, h, i: (b, h, 0))],
        out_specs=pl.BlockSpec((1, tq, heads * HEAD_DIM), lambda b, h, i: (b, i, h)),
        out_shape=jax.ShapeDtypeStruct((B, T, FOX_W), BF16),
        scratch_shapes=_flash_scratch(heads, tq, tk),
        compiler_params=_cparams(("arbitrary", "arbitrary", "arbitrary")),
        name="fox",
    )(qat, ka, vat)


def _cmp_kernel(r_ref, w1_ref, pos_ref, b1_ref, w2_ref, b2_ref, rc_ref, rs1_ref, rs2_ref, o_ref, ot_ref):
    nr = o_ref.shape[0]
    half = CMP_BLOCK // 2
    a = bm = None
    for l in range(half):
        x = r_ref[pl.ds(l, nr, stride=CMP_STRIDE), :]
        ta = _dot((x + pos_ref[l:l + 1, :]).astype(BF16), w1_ref[0, l * KV_W:(l + 1) * KV_W, :])
        tb = _dot((x + pos_ref[half + l:half + l + 1, :]).astype(BF16), w1_ref[1, l * KV_W:(l + 1) * KV_W, :])
        a = ta if a is None else a + ta
        bm = tb if bm is None else bm + tb
    hdn = a + pltpu.roll(bm, nr - 1, 0) + b1_ref[...]
    act = 0.5 * hdn * (1.0 + jnp.tanh(0.7978845608028654 * (hdn + 0.044715 * (hdn * hdn * hdn))))
    y = _rope(_dot(act.astype(BF16), w2_ref[...]) + b2_ref[...], rc_ref[...], rs1_ref[...], rs2_ref[...])
    o_ref[...] = y.astype(BF16)
    ot_ref[...] = y.T.astype(BF16)


def _cmp(layer, kvc, w1, pos, b1, w2, b2, rc, rs1, rs2):
    _, B, T, _ = kvc.shape
    nr = T // CMP_STRIDE
    rw = CMP_STRIDE * KV_W
    per_kv = lambda shape: pl.BlockSpec((None, None) + shape, lambda s, b: (layer, s) + (0,) * len(shape))
    tab = pl.BlockSpec((None, nr, LANES), lambda s, b: (s, 0, 0))
    return pl.pallas_call(
        _cmp_kernel,
        grid=(2, B),
        in_specs=[pl.BlockSpec((None, None, T, KV_W), lambda s, b: (s, b, 0, 0)),
                  per_kv((2, rw, 2 * CMP_HIDDEN)), per_kv((CMP_BLOCK, KV_W)), per_kv((1, 2 * CMP_HIDDEN)),
                  per_kv((2 * CMP_HIDDEN, KV_W)), per_kv((1, KV_W)), tab, tab, tab],
        out_specs=[pl.BlockSpec((None, None, nr, KV_W), lambda s, b: (s, b, 0, 0)),
                   pl.BlockSpec((None, None, KV_W, nr), lambda s, b: (s, b, 0, 0))],
        out_shape=[jax.ShapeDtypeStruct((2, B, nr, KV_W), BF16), jax.ShapeDtypeStruct((2, B, KV_W, nr), BF16)],
        compiler_params=_cparams(("arbitrary", "arbitrary")),
        name="cmp",
    )(kvc, w1, pos, b1, w2, b2, rc, rs1, rs2)


def _group_query_weights(qt, g):
    zero = jnp.zeros((HEAD_DIM, qt.shape[1]), qt.dtype)
    out = []
    for r in range(NSA_GROUP):
        q = qt[r * HEAD_DIM:(r + 1) * HEAD_DIM, :]
        out.append(jnp.concatenate([jnp.where(g == 0, q, zero), jnp.where(g == 1, q, zero)], axis=0))
    return out


def _cmpsel_kernel(qt_ref, kc_ref, vct_ref, mapt_ref, oc_ref, mt_ref, *, tq, n_sel, top):
    g = pl.program_id(1)
    i = pl.program_id(2)
    nr = kc_ref.shape[2]
    kc = kc_ref[0, 0]
    vct = vct_ref[0, 0]
    cend = lax.broadcasted_iota(jnp.int32, (nr, tq), 0) * CMP_STRIDE + (CMP_BLOCK - 1)
    qpos = i * tq + lax.broadcasted_iota(jnp.int32, (nr, tq), 1)
    valid = cend <= qpos
    sts = [_dot(kc, w) for w in _group_query_weights(qt_ref[0], g)]
    vct_ones = _with_ones_rows(vct)
    imp = jnp.zeros((LANES, tq), F32)
    outs = []
    for st in sts:
        st = jnp.where(valid, st, NEG_BIG)
        e = jnp.where(valid, jnp.exp2(st - jnp.max(st, axis=0, keepdims=True)), 0.0).astype(BF16)
        oe = _dot(vct_ones, e)
        den = oe[HEAD_DIM:HEAD_DIM + 1, :]
        inv = jnp.where(den > 0.0, 1.0 / jnp.where(den > 0.0, den, 1.0), 0.0)
        outs.append(oe[0:HEAD_DIM, :] * inv)
        imp = imp + _dot(mapt_ref[...], e) * inv
    oc_ref[0] = jnp.concatenate(outs, axis=0).T

    sid = lax.broadcasted_iota(jnp.int32, (LANES, tq), 0)
    qp = i * tq + lax.broadcasted_iota(jnp.int32, (LANES, tq), 1)
    score0 = jnp.where((sid * SEL_BLOCK > qp) | (sid >= n_sel), -3.0,
                       jnp.where(sid == (qp >> 6), FORCE_SCORE, jnp.where(sid == 0, 2.0 * FORCE_SCORE, imp)))

    score = score0
    for _ in range(top):
        score = jnp.where(score == jnp.max(score, axis=0, keepdims=True), -2.0, score)
    picked = score == -2.0
    mt_ref[0, 0] = jnp.where(picked, 0.0, NEG_BIG).astype(BF16)
    n_picked = jnp.sum(jnp.where(picked, 1.0, 0.0), axis=0, keepdims=True)

    @pl.when(jnp.max(n_picked) > top)
    def _():
        sidf = sid.astype(F32)
        score = score0
        maskt = jnp.full((LANES, tq), NEG_BIG, F32)
        for _ in range(top):
            mx = jnp.max(score, axis=0, keepdims=True)
            idx = jnp.min(jnp.where(score == mx, sidf, float(LANES)), axis=0, keepdims=True)
            pick = (sidf == idx) & (mx > -3.0)
            maskt = jnp.where(pick, 0.0, maskt)
            score = jnp.where(pick, -2.0, score)
        mt_ref[0, 0] = maskt.astype(BF16)


def _cmpsel(qbt, kc, vct, selmap_t):
    B, _, T = qbt.shape
    nr = kc.shape[2]
    tq = NSA_TQ
    n_sel = T // SEL_BLOCK
    return pl.pallas_call(
        functools.partial(_cmpsel_kernel, tq=tq, n_sel=n_sel, top=min(SEL_TOPK, n_sel)),
        grid=(B, NSA_KV_HEADS, T // tq),
        in_specs=[pl.BlockSpec((1, GROUP_W, tq), lambda b, g, i: (b, g, i)),
                  pl.BlockSpec((1, 1, nr, KV_W), lambda b, g, i: (0, b, 0, 0)),
                  pl.BlockSpec((1, 1, HEAD_DIM, nr), lambda b, g, i: (1, b, g, 0)),
                  pl.BlockSpec((LANES, nr), lambda b, g, i: (0, 0))],
        out_specs=[pl.BlockSpec((1, tq, GROUP_W), lambda b, g, i: (b, i, g)),
                   pl.BlockSpec((1, 1, LANES, tq), lambda b, g, i: (b, g, 0, i))],
        out_shape=[jax.ShapeDtypeStruct((B, T, NSA_W), F32),
                   jax.ShapeDtypeStruct((B, NSA_KV_HEADS, LANES, T), BF16)],
        compiler_params=_cparams(("arbitrary", "arbitrary", "arbitrary")),
        name="cmpsel",
    )(qbt, kc, vct, selmap_t)


def _sel_kernel(qt_ref, mt_ref, k_ref, vt_ref, kw_ref, vwt_ref, os_ref, ow_ref, w_scr, *scr, tq, tk):
    g = pl.program_id(1)
    for r, w in enumerate(_group_query_weights(qt_ref[0], g)):
        w_scr[r] = jnp.concatenate([w, mt_ref[0, 0]], axis=0)

    def score_fn(r, start):
        return _dot(k_ref[0, pl.ds(start, tk), :], w_scr[r])

    def vt_fn(r, start):
        return vt_ref[0, :, pl.ds(start, tk)]

    def w_score_fn(r, start):
        return _dot(kw_ref[0, pl.ds(start, tk), :], w_scr[r, 0:KV_W, :])

    def w_vt_fn(r, start):
        return vwt_ref[0, :, pl.ds(start, tk)]

    ost, owt = _causal_flash_t(pl.program_id(2), NSA_GROUP, tq, tk, score_fn, vt_fn, scr,
                               window=(w_score_fn, w_vt_fn))
    os_ref[0] = ost.T
    ow_ref[0] = owt.T


def _sel(qbt, maskt, ks, vst, kw, vwt):
    B, _, T = qbt.shape
    tq, tk = min(SEL_TQ, T), min(SEL_TK, T)
    kaug = KV_W + LANES
    out = pl.BlockSpec((1, tq, GROUP_W), lambda b, g, i: (b, i, g))
    return pl.pallas_call(
        functools.partial(_sel_kernel, tq=tq, tk=tk),
        grid=(B, NSA_KV_HEADS, T // tq),
        in_specs=[pl.BlockSpec((1, GROUP_W, tq), lambda b, g, i: (b, g, i)),
                  pl.BlockSpec((1, 1, LANES, tq), lambda b, g, i: (b, g, 0, i)),
                  pl.BlockSpec((1, T, kaug), lambda b, g, i: (b, 0, 0)),
                  pl.BlockSpec((1, HEAD_DIM, T), lambda b, g, i: (b, g, 0)),
                  pl.BlockSpec((1, T, KV_W), lambda b, g, i: (b, 0, 0)),
                  pl.BlockSpec((1, HEAD_DIM, T), lambda b, g, i: (b, g, 0))],
        out_specs=[out, out],
        out_shape=[jax.ShapeDtypeStruct((B, T, NSA_W), F32)] * 2,
        scratch_shapes=[pltpu.VMEM((NSA_GROUP, kaug, tq), BF16)] + _flash_scratch(NSA_GROUP, tq, tk),
        compiler_params=_cparams(("arbitrary", "arbitrary", "arbitrary")),
        name="sel",
    )(qbt, maskt, ks, vst, kw, vwt)


def _merge_kernel(x_ref, g_ref, oa_ref, oc_ref, os_ref, ow_ref, gb_ref, wg_ref, wof_ref, won_ref, wout_ref, e_ref,
                  o_ref):
    x = x_ref[...]
    h = _rmsnorm(x, g_ref[...]).astype(BF16)
    ya = _dot(oa_ref[...], wof_ref[...])
    sg = _pack3(_sigmoid(gb_ref[...]), 3 * NSA_HEADS)
    on = (_dot(sg, e_ref[0]) * oc_ref[...] + _dot(sg, e_ref[1]) * os_ref[...]
          + _dot(sg, e_ref[2]) * ow_ref[...])
    yb = _dot(on.astype(BF16), won_ref[...])
    mixed = (_sigmoid(_dot(h, wg_ref[:, 0:D_MODEL])) * ya
             + _sigmoid(_dot(h, wg_ref[:, D_MODEL:2 * D_MODEL])) * yb)
    o_ref[...] = x + _dot(mixed.astype(BF16), wout_ref[...])


def _layer_spec(layer, shape):
    return pl.BlockSpec((None,) + shape, lambda *_: (layer,) + (0,) * len(shape))


def _merge(layer, x, g, oa, oc, os_, ow, gb, wp, wof, won, wout, e):
    n = x.shape[0]
    tm = MERGE_TM
    row = lambda w: pl.BlockSpec((tm, w), lambda t: (t, 0))
    const = lambda shape: pl.BlockSpec(shape, lambda t: (0,) * len(shape))
    assert OFF_GATE % (2 * D_MODEL) == 0
    gate_w = pl.BlockSpec((None, D_MODEL, 2 * D_MODEL), lambda t: (layer, 0, OFF_GATE // (2 * D_MODEL)))
    return pl.pallas_call(
        _merge_kernel,
        grid=(n // tm,),
        in_specs=[row(D_MODEL), _layer_spec(layer, (1, D_MODEL)), row(FOX_W), row(NSA_W), row(NSA_W), row(NSA_W),
                  row(LANES), gate_w,
                  _layer_spec(layer, (FOX_W, D_MODEL)), _layer_spec(layer, (NSA_W, D_MODEL)),
                  _layer_spec(layer, (D_MODEL, D_MODEL)), const((3, LANES, NSA_W))],
        out_specs=row(D_MODEL),
        out_shape=jax.ShapeDtypeStruct((n, D_MODEL), F32),
        compiler_params=_cparams(("arbitrary",)),
        name="merge",
    )(x, g, oa, oc, os_, ow, gb, wp, wof, won, wout, e)


def _mlp_kernel(x_ref, g_ref, wu_ref, wd_ref, gf_ref, o_ref, h_scr, *, final_norm):
    x = x_ref[...]
    h_scr[...] = _rmsnorm(x, g_ref[...]).astype(BF16)
    acc = None
    for c0 in range(0, D_FF, MLP_FF_CHUNK):
        u = jnp.maximum(_dot(h_scr[...], wu_ref[:, c0:c0 + MLP_FF_CHUNK]), 0.0)
        part = _dot((u * u).astype(BF16), wd_ref[c0:c0 + MLP_FF_CHUNK, :])
        acc = part if acc is None else acc + part
    y = x + acc
    o_ref[...] = _rmsnorm(y, gf_ref[...]) if final_norm else y


def _mlp(layer, x, g, wu, wd, gf, final_norm):
    n = x.shape[0]
    tm = MLP_TM
    row = pl.BlockSpec((tm, D_MODEL), lambda t: (t, 0))
    const = lambda shape: pl.BlockSpec(shape, lambda t: (0,) * len(shape))
    return pl.pallas_call(
        functools.partial(_mlp_kernel, final_norm=final_norm),
        grid=(n // tm,),
        in_specs=[row, _layer_spec(layer, (1, D_MODEL)), _layer_spec(layer, (D_MODEL, D_FF)),
                  _layer_spec(layer, (D_FF, D_MODEL)), const((1, D_MODEL))],
        out_specs=row,
        out_shape=jax.ShapeDtypeStruct((n, D_MODEL), F32),
        scratch_shapes=[pltpu.VMEM((tm, D_MODEL), BF16)],
        compiler_params=_cparams(("arbitrary",)),
        name="mlp",
    )(x, g, wu, wd, gf)


def _rope_tables(pos):
    inv = ROPE_THETA ** (-jnp.arange(ROPE_HALF, dtype=F32) / ROPE_HALF)
    ang = pos.astype(F32)[:, None] * inv[None, :]
    cos, sin = jnp.cos(ang), jnp.sin(ang)
    n = pos.shape[0]
    rest = HEAD_DIM - ROPE_DIM
    rc = jnp.concatenate([cos, cos, jnp.ones((n, rest), F32)], axis=1)
    rs1 = jnp.concatenate([-sin, jnp.zeros((n, HEAD_DIM - ROPE_HALF), F32)], axis=1)
    rs2 = jnp.concatenate([jnp.zeros((n, ROPE_HALF), F32), sin, jnp.zeros((n, rest), F32)], axis=1)
    two = lambda t: jnp.concatenate([t, t], axis=1)
    return two(rc), two(rs1), two(rs2)


def _pad_heads(w, n_heads):
    lead = w.shape[:-1]
    w = w.reshape(lead + (n_heads, HEAD_DIM))
    return jnp.concatenate([w, jnp.zeros_like(w)], axis=-1).reshape(lead + (n_heads * LANES,))


def _pad_cols(w, width):
    return jnp.concatenate([w, jnp.zeros(w.shape[:-1] + (width - w.shape[-1],), w.dtype)], axis=-1)


def _proj_weight(w_in):
    o = np.cumsum([0, FOX_W, FOX_W, FOX_W, FOX_HEADS, NSA_W, 6 * KV_W, 3 * NSA_HEADS, D_MODEL, D_MODEL])
    cols = lambda j: w_in[..., o[j]:o[j + 1]]
    qscale = SCALE * LOG2E
    parts = [
        _pad_heads(cols(0) * qscale, FOX_HEADS), _pad_heads(cols(1), FOX_HEADS), cols(2),
        _pad_cols(cols(3), LANES),
        cols(4) * qscale,
        cols(5),
        _pad_cols(cols(6), LANES),
        cols(7), cols(8),
    ]
    return jnp.concatenate(parts, axis=-1).astype(BF16)


def _aug_placement():
    pq = np.zeros((LANES, FOX_PAD_W), np.float32)
    pk = np.zeros((LANES, FOX_PAD_W), np.float32)
    for h in range(FOX_HEADS):
        for piece in range(3):
            pq[piece * PIECE_LANES + h, h * LANES + AUG_Q_C + piece] = 1.0
            pk[piece * PIECE_LANES + h, h * LANES + AUG_K_C + piece] = -1.0
    return jnp.asarray(pq, BF16), jnp.asarray(pk, BF16)


def _gate_placement():
    e = np.zeros((3, LANES, NSA_W), np.float32)
    for j in range(3):
        for h in range(NSA_HEADS):
            for piece in range(3):
                e[j, piece * PIECE_LANES + 3 * h + j, h * HEAD_DIM:(h + 1) * HEAD_DIM] = 1.0
    return jnp.asarray(e, BF16)


def _sel_map_t(T, nr):
    nc = (T - CMP_BLOCK) // CMP_STRIDE + 1
    n_sel = T // SEL_BLOCK
    cs = np.arange(nc) * CMP_STRIDE
    ce = cs + CMP_BLOCK
    ss = np.arange(n_sel) * SEL_BLOCK
    se = ss + SEL_BLOCK
    ov = np.clip(np.minimum(ce[None, :], se[:, None]) - np.maximum(cs[None, :], ss[:, None]), 0, None)
    m = np.zeros((LANES, nr), np.float32)
    m[:n_sel, :nc] = ov / CMP_BLOCK
    return jnp.asarray(m, BF16)


def _cmp_weights(cmp_pos, cmp_w1, cmp_b1, cmp_w2, cmp_b2):
    d, half, g2 = cmp_w1.shape[0], CMP_BLOCK // 2, NSA_KV_HEADS

    def group_diag(w, axis):
        z = jnp.zeros_like(w)
        return jnp.stack([jnp.concatenate([w, z], axis=-1), jnp.concatenate([z, w], axis=-1)], axis=axis)

    w1 = group_diag(cmp_w1.astype(BF16).reshape(d, 2, 2, half, HEAD_DIM, CMP_HIDDEN), 4)
    w1 = w1.reshape(d, 2, 2, half * KV_W, g2 * CMP_HIDDEN)
    pos = jnp.concatenate([cmp_pos] * g2, axis=-1)
    b1 = jnp.concatenate([cmp_b1] * g2, axis=-1)[:, :, None, :]
    w2 = group_diag(cmp_w2.astype(BF16), 2).reshape(d, 2, g2 * CMP_HIDDEN, KV_W)
    b2 = jnp.concatenate([cmp_b2] * g2, axis=-1)[:, :, None, :]
    return w1, pos, b1, w2, b2


def kernel(x, norm_mix, w_in, b_forget, cmp_pos, cmp_w1, cmp_b1, cmp_w2, cmp_b2, w_o_fox, w_o_nsa, w_out,
           norm_mlp, w_up, w_down, norm_final):
    B, T, _ = x.shape
    depth = w_in.shape[0]
    nr = T // CMP_STRIDE
    n = B * T
    flat = lambda a: a.reshape(n, a.shape[-1])

    pq, pk = _aug_placement()
    gate_e = _gate_placement()
    selmap_t = _sel_map_t(T, nr)
    rope = _rope_tables(jnp.arange(T))
    crope = [jnp.stack([t, jnp.full_like(t, fill)]) for t, fill in
             zip(_rope_tables(jnp.arange(nr) * CMP_STRIDE + CMP_BLOCK - 1), (1.0, 0.0, 0.0))]
    wp = _proj_weight(w_in)
    bf = _pad_cols(b_forget, LANES)[:, None, :]
    cmp_w = _cmp_weights(cmp_pos, cmp_w1, cmp_b1, cmp_w2, cmp_b2)
    wof, won, wout = w_o_fox.astype(BF16), w_o_nsa.astype(BF16), w_out.astype(BF16)
    wu, wd = w_up.astype(BF16), w_down.astype(BF16)
    g_mix, g_mlp = norm_mix[:, None, :], norm_mlp[:, None, :]

    for l in range(depth):
        qat, ka, vat, qbt, ks, kw, vst, vwt, kvc, gb = _proj(l, x, g_mix, wp, bf, pq, pk, *rope)
        oa = _fox(qat, ka, vat)
        cmp_rows, cmp_cols = _cmp(l, kvc, *cmp_w, *crope)
        oc, maskt = _cmpsel(qbt, cmp_rows, cmp_cols, selmap_t)
        os_, ow = _sel(qbt, maskt, ks, vst, kw, vwt)
        x1 = _merge(l, flat(x), g_mix, flat(oa), flat(oc), flat(os_), flat(ow), flat(gb), wp,
                    wof, won, wout, gate_e)
        x = _mlp(l, x1, g_mlp, wu, wd, norm_final[None, :], l == depth - 1).reshape(B, T, D_MODEL)
    return x
```

```python
import functools

import numpy as np
import jax
import jax.numpy as jnp
from jax import lax
from jax.experimental import pallas as pl
from jax.experimental.pallas import tpu as pltpu

F32 = jnp.float32
BF16 = jnp.bfloat16

D_MODEL = 1024
HEAD_DIM = 64
FOX_HEADS = 8
NSA_HEADS = 8
NSA_KV_HEADS = 2
NSA_GROUP = NSA_HEADS // NSA_KV_HEADS
CMP_BLOCK = 32
CMP_STRIDE = 16
CMP_HIDDEN = 4 * HEAD_DIM
SEL_BLOCK = 64
SEL_TOPK = 16
WINDOW = 512
ROPE_THETA = 500000.0
ROPE_DIM = HEAD_DIM // 4
ROPE_HALF = ROPE_DIM // 2
D_FF = 4 * D_MODEL
RMS_EPS = 1e-6
FORCE_SCORE = 1e6
NEG_BIG = -1e30
SCALE = HEAD_DIM ** -0.5
LOG2E = 1.4426950408889634

FOX_W = FOX_HEADS * HEAD_DIM
NSA_W = NSA_HEADS * HEAD_DIM
KV_W = NSA_KV_HEADS * HEAD_DIM

LANES = 128
DEN_ROWS = 16
PIECE_LANES = 32
VMEM_LIMIT_BYTES = 56 * 1024 * 1024

FOX_PAD_W = FOX_HEADS * LANES
GROUP_W = NSA_GROUP * HEAD_DIM
OFF_QA = 0
OFF_KA = OFF_QA + FOX_PAD_W
OFF_VA = OFF_KA + FOX_PAD_W
OFF_F = OFF_VA + FOX_W
OFF_QB = OFF_F + LANES
OFF_KC = OFF_QB + NSA_W
OFF_VC = OFF_KC + KV_W
OFF_KS = OFF_VC + KV_W
OFF_VS = OFF_KS + KV_W
OFF_KW = OFF_VS + KV_W
OFF_VW = OFF_KW + KV_W
OFF_GB = OFF_VW + KV_W
OFF_GATE = OFF_GB + LANES
PROJ_W = OFF_GATE + 2 * D_MODEL

AUG_Q_C = HEAD_DIM
AUG_K_C = HEAD_DIM + 3

PROJ_TM = 256
FOX_TQ = 512
FOX_TK = 512
FOX_HEADS_PER_STEP = 4
NSA_TQ = 512
SEL_TQ = 512
SEL_TK = 512
MERGE_TM = 256
MLP_TM = 512
MLP_FF_CHUNK = 512


def _cparams(sem):
    return pltpu.CompilerParams(dimension_semantics=sem, vmem_limit_bytes=VMEM_LIMIT_BYTES)


def _dot(a, b):
    return jnp.dot(a, b, preferred_element_type=F32)


def _split3(x):
    hi = x.astype(BF16)
    r = x - hi.astype(F32)
    mid = r.astype(BF16)
    lo = (r - mid.astype(F32)).astype(BF16)
    return hi, mid, lo


def _pack3(x, n_real):
    lane = lax.broadcasted_iota(jnp.int32, x.shape, 1)
    hi, mid, lo = _split3(jnp.where(lane < n_real, x, 0.0))
    return (hi.astype(F32) + pltpu.roll(mid.astype(F32), PIECE_LANES, 1)
            + pltpu.roll(lo.astype(F32), 2 * PIECE_LANES, 1)).astype(BF16)


def _lane_tile(x, reps):
    return x if reps == 1 else jnp.concatenate([x] * reps, axis=1)


def _rope(y, rc, rs1, rs2):
    n = y.shape[1]
    return y * rc + pltpu.roll(y, n - ROPE_HALF, 1) * rs1 + pltpu.roll(y, ROPE_HALF, 1) * rs2


def _rmsnorm(x, g):
    return x * lax.rsqrt(jnp.mean(x * x, axis=-1, keepdims=True) + RMS_EPS) * g


def _sigmoid(x):
    return 1.0 / (1.0 + jnp.exp(-x))


def _proj_kernel(x_ref, g_ref, w_ref, bf_ref, pq_ref, pk_ref, rc_ref, rs1_ref, rs2_ref,
                 qa_ref, ka_ref, va_ref, qb_ref, ks_ref, kw_ref, vs_ref, vw_ref, kvc_ref, gb_ref,
                 h_scr, carry_scr, *, tm):
    t = pl.program_id(1)

    @pl.when(t == 0)
    def _():
        carry_scr[...] = jnp.zeros_like(carry_scr)

    h_scr[...] = _rmsnorm(x_ref[0], g_ref[...]).astype(BF16)

    def seg(c0, n):
        return _dot(h_scr[...], w_ref[:, c0:c0 + n])

    f = seg(OFF_F, LANES) + bf_ref[...]
    lf = jnp.minimum(f, 0.0) - jnp.log1p(jnp.exp(-jnp.abs(f)))
    row = lax.broadcasted_iota(jnp.int32, (tm, tm), 0)
    col = lax.broadcasted_iota(jnp.int32, (tm, tm), 1)
    tri = jnp.where(col <= row, 1.0, 0.0).astype(BF16)
    hi, mid, lo = _split3(lf)
    c = _dot(tri, hi) + _dot(tri, mid) + _dot(tri, lo) + carry_scr[0:1, :]
    carry_scr[0:1, :] = c[tm - 1:tm, :]
    c3 = _pack3(c * LOG2E, FOX_HEADS)

    lane = lax.broadcasted_iota(jnp.int32, (tm, FOX_PAD_W), 1) & (LANES - 1)
    ones_q = jnp.where((lane >= AUG_K_C) & (lane < AUG_K_C + 3), 1.0, 0.0)
    ones_k = jnp.where((lane >= AUG_Q_C) & (lane < AUG_Q_C + 3), 1.0, 0.0)
    qa_ref[0] = (seg(OFF_QA, FOX_PAD_W) + _dot(c3, pq_ref[...]) + ones_q).T.astype(BF16)
    ka_ref[0] = (seg(OFF_KA, FOX_PAD_W) + _dot(c3, pk_ref[...]) + ones_k).astype(BF16)
    va_ref[0] = seg(OFF_VA, FOX_W).T.astype(BF16)

    rc, rs1, rs2 = rc_ref[...], rs1_ref[...], rs2_ref[...]
    reps = NSA_W // LANES
    qb_ref[0] = _rope(seg(OFF_QB, NSA_W), _lane_tile(rc, reps), _lane_tile(rs1, reps),
                      _lane_tile(rs2, reps)).T.astype(BF16)
    pos = t * tm + lax.broadcasted_iota(jnp.int32, (tm, LANES), 0)
    onehot = jnp.where((pos >> 6) == lax.broadcasted_iota(jnp.int32, (tm, LANES), 1), 1.0, 0.0)
    ks_ref[0] = jnp.concatenate([_rope(seg(OFF_KS, KV_W), rc, rs1, rs2), onehot], axis=1).astype(BF16)
    kw_ref[0] = _rope(seg(OFF_KW, KV_W), rc, rs1, rs2).astype(BF16)
    vs_ref[0] = seg(OFF_VS, KV_W).T.astype(BF16)
    vw_ref[0] = seg(OFF_VW, KV_W).T.astype(BF16)
    kvc_ref[0, 0] = seg(OFF_KC, KV_W)
    kvc_ref[1, 0] = seg(OFF_VC, KV_W)
    gb_ref[0] = seg(OFF_GB, LANES)


def _proj(layer, x, g, wp, bf, pq, pk, rc, rs1, rs2):
    B, T, _ = x.shape
    tm = PROJ_TM
    row = lambda w: pl.BlockSpec((1, tm, w), lambda b, t: (b, t, 0))
    col = lambda w: pl.BlockSpec((1, w, tm), lambda b, t: (b, 0, t))
    const = lambda shape: pl.BlockSpec(shape, lambda b, t: (0,) * len(shape))
    tab = pl.BlockSpec((tm, LANES), lambda b, t: (t, 0))
    out_shapes = [
        jax.ShapeDtypeStruct((B, FOX_PAD_W, T), BF16),
        jax.ShapeDtypeStruct((B, T, FOX_PAD_W), BF16),
        jax.ShapeDtypeStruct((B, FOX_W, T), BF16),
        jax.ShapeDtypeStruct((B, NSA_W, T), BF16),
        jax.ShapeDtypeStruct((B, T, KV_W + LANES), BF16),
        jax.ShapeDtypeStruct((B, T, KV_W), BF16),
        jax.ShapeDtypeStruct((B, KV_W, T), BF16),
        jax.ShapeDtypeStruct((B, KV_W, T), BF16),
        jax.ShapeDtypeStruct((2, B, T, KV_W), F32),
        jax.ShapeDtypeStruct((B, T, LANES), F32),
    ]
    out_specs = [
        col(FOX_PAD_W), row(FOX_PAD_W), col(FOX_W), col(NSA_W),
        row(KV_W + LANES), row(KV_W), col(KV_W), col(KV_W),
        pl.BlockSpec((2, 1, tm, KV_W), lambda b, t: (0, b, t, 0)),
        row(LANES),
    ]
    return pl.pallas_call(
        functools.partial(_proj_kernel, tm=tm),
        grid=(B, T // tm),
        in_specs=[row(D_MODEL), _layer_spec(layer, (1, D_MODEL)), _layer_spec(layer, (D_MODEL, OFF_GATE)),
                  _layer_spec(layer, (1, LANES)),
                  const((LANES, FOX_PAD_W)), const((LANES, FOX_PAD_W)), tab, tab, tab],
        out_specs=out_specs,
        out_shape=out_shapes,
        scratch_shapes=[pltpu.VMEM((tm, D_MODEL), BF16), pltpu.VMEM((8, LANES), F32)],
        compiler_params=_cparams(("arbitrary", "arbitrary")),
        name="proj",
    )(x, g, wp, bf, pq, pk, rc, rs1, rs2)


def _with_ones_rows(vt):
    first = lax.broadcasted_iota(jnp.int32, (DEN_ROWS, vt.shape[1]), 0) == 0
    return jnp.concatenate([vt, jnp.where(first, 1.0, 0.0).astype(vt.dtype)], axis=0)


def _causal_flash_t(i, n_heads, tq, tk, score_fn, vt_fn, scr, window=None):
    s0_scr, s1_scr, m_scr, acc_scr = scr
    n_full = (i * tq) // tk

    def scores(j, s_scr, fn=score_fn):
        start = pl.multiple_of(j * tk, tk)
        for h in range(n_heads):
            s_scr[h] = fn(h, start)

    def softmax_pv(j, s_scr):
        start = pl.multiple_of(j * tk, tk)
        for h in range(n_heads):
            st = s_scr[h]
            m = m_scr[h]
            m_new = jnp.maximum(m, jnp.max(st, axis=0, keepdims=True))
            p = jnp.exp2(st - m_new).astype(BF16)
            acc_scr[h] = jnp.exp2(m - m_new) * acc_scr[h] + _dot(_with_ones_rows(vt_fn(h, start)), p)
            m_scr[h] = m_new

    def last_tile():
        half = tk // 2
        start = pl.multiple_of(n_full * tk, tk)
        tri = (lax.broadcasted_iota(jnp.int32, (half, tq), 0) <= lax.broadcasted_iota(jnp.int32, (half, tq), 1))
        neg = jnp.full((1, half), NEG_BIG, F32)
        pad = jnp.zeros((HEAD_DIM + DEN_ROWS, half), F32)
        for h in range(n_heads):
            top = jnp.where(tri, s0_scr[h, 0:half, :], NEG_BIG)
            bot = jnp.where(tri[:, 0:half], s0_scr[h, half:tk, half:tq], NEG_BIG)
            m = m_scr[h]
            m_bot = jnp.concatenate([neg, jnp.max(bot, axis=0, keepdims=True)], axis=1)
            m_new = jnp.maximum(jnp.maximum(m, jnp.max(top, axis=0, keepdims=True)), m_bot)
            vt = _with_ones_rows(vt_fn(h, start))
            upd = _dot(vt[:, 0:half], jnp.exp2(top - m_new).astype(BF16))
            upd_right = _dot(vt[:, half:tk], jnp.exp2(bot - m_new[:, half:tq]).astype(BF16))
            acc_scr[h] = (jnp.exp2(m - m_new) * acc_scr[h] + upd + jnp.concatenate([pad, upd_right], axis=1))
            m_scr[h] = m_new

    def normalised(acc_ref):
        return jnp.concatenate([acc_ref[h, 0:HEAD_DIM, :] / acc_ref[h, HEAD_DIM:HEAD_DIM + 1, :]
                                for h in range(n_heads)], axis=0)

    assert tq == tk
    m_scr[...] = jnp.full_like(m_scr, NEG_BIG)
    acc_scr[...] = jnp.zeros_like(acc_scr)

    odd = n_full & 1

    @pl.when(odd == 1)
    def _():
        scores(0, s1_scr)
        scores(1, s0_scr)
        softmax_pv(0, s1_scr)

    @pl.when(odd == 0)
    def _():
        scores(0, s0_scr)

    def pair(t0):
        scores(t0 + 1, s1_scr)
        softmax_pv(t0, s0_scr)
        scores(t0 + 2, s0_scr)
        softmax_pv(t0 + 1, s1_scr)

    n_pairs = n_full // 2
    odd_pair = n_pairs & 1

    @pl.when(odd_pair == 1)
    def _():
        pair(odd)

    def body(jq, carry):
        t0 = odd + 2 * odd_pair + 4 * jq
        pair(t0)
        pair(t0 + 2)
        return carry

    lax.fori_loop(0, n_pairs // 2, body, 0)
    if window is None:
        last_tile()
        return normalised(acc_scr), None

    assert tq == tk == WINDOW
    w_score, w_vt = window
    prev = jnp.maximum(i - 1, 0)
    first_row = jnp.where(i > 0, 0, tk + tq)
    scores(i, s1_scr, w_score)
    last_tile()
    scores(prev, s0_scr, w_score)
    rel = (lax.broadcasted_iota(jnp.int32, (tk, tq), 0) - lax.broadcasted_iota(jnp.int32, (tk, tq), 1))
    cur, start, start_prev = rel <= 0, pl.multiple_of(i * tk, tk), pl.multiple_of(prev * tk, tk)
    outs = []
    for h in range(n_heads):
        st = jnp.where(cur, s1_scr[h], jnp.where(rel > first_row, s0_scr[h], NEG_BIG))
        e = jnp.exp2(st - jnp.max(st, axis=0, keepdims=True))
        oe = (_dot(_with_ones_rows(w_vt(h, start)), jnp.where(cur, e, 0.0).astype(BF16))
              + _dot(_with_ones_rows(w_vt(h, start_prev)), jnp.where(cur, 0.0, e).astype(BF16)))
        outs.append(oe[0:HEAD_DIM, :] / oe[HEAD_DIM:HEAD_DIM + 1, :])
    return normalised(acc_scr), jnp.concatenate(outs, axis=0)


def _flash_scratch(n_heads, tq, tk):
    return [pltpu.VMEM((n_heads, tk, tq), F32), pltpu.VMEM((n_heads, tk, tq), F32),
            pltpu.VMEM((n_heads, 1, tq), F32), pltpu.VMEM((n_heads, HEAD_DIM + DEN_ROWS, tq), F32)]


def _fox_kernel(qt_ref, k_ref, vt_ref, o_ref, *scr, tq, tk, heads):
    def score_fn(h, start):
        return _dot(k_ref[0, pl.ds(start, tk), h * LANES:(h + 1) * LANES], qt_ref[0, h * LANES:(h + 1) * LANES, :])

    def vt_fn(h, start):
        return vt_ref[0, h * HEAD_DIM:(h + 1) * HEAD_DIM, pl.ds(start, tk)]

    ot, _ = _causal_flash_t(pl.program_id(2), heads, tq, tk, score_fn, vt_fn, scr)
    o_ref[0] = ot.T.astype(BF16)


def _fox(qat, ka, vat):
    B, T, _ = ka.shape
    tq, tk = min(FOX_TQ, T), min(FOX_TK, T)
    heads = FOX_HEADS_PER_STEP
    return pl.pallas_call(
        functools.partial(_fox_kernel, tq=tq, tk=tk, heads=heads),
        grid=(B, FOX_HEADS // heads, T // tq),
        in_specs=[pl.BlockSpec((1, heads * LANES, tq), lambda b, h, i: (b, h, i)),
                  pl.BlockSpec((1, T, heads * LANES), lambda b, h, i: (b, 0, h)),
                  pl.BlockSpec((1, heads * HEAD_DIM, T), lambda b, h, i: (b, h, 0))],
        out_specs=pl.BlockSpec((1, tq, heads * HEAD_DIM), lambda b, h, i: (b, i, h)),
        out_shape=jax.ShapeDtypeStruct((B, T, FOX_W), BF16),
        scratch_shapes=_flash_scratch(heads, tq, tk),
        compiler_params=_cparams(("arbitrary", "arbitrary", "arbitrary")),
        name="fox",
    )(qat, ka, vat)


def _cmp_kernel(r_ref, w1_ref, pos_ref, b1_ref, w2_ref, b2_ref, rc_ref, rs1_ref, rs2_ref, o_ref, ot_ref):
    nr = o_ref.shape[0]
    half = CMP_BLOCK // 2
    a = bm = None
    for l in range(half):
        x = r_ref[pl.ds(l, nr, stride=CMP_STRIDE), :]
        ta = _dot((x + pos_ref[l:l + 1, :]).astype(BF16), w1_ref[0, l * KV_W:(l + 1) * KV_W, :])
        tb = _dot((x + pos_ref[half + l:half + l + 1, :]).astype(BF16), w1_ref[1, l * KV_W:(l + 1) * KV_W, :])
        a = ta if a is None else a + ta
        bm = tb if bm is None else bm + tb
    hdn = a + pltpu.roll(bm, nr - 1, 0) + b1_ref[...]
    act = 0.5 * hdn * (1.0 + jnp.tanh(0.7978845608028654 * (hdn + 0.044715 * (hdn * hdn * hdn))))
    y = _rope(_dot(act.astype(BF16), w2_ref[...]) + b2_ref[...], rc_ref[...], rs1_ref[...], rs2_ref[...])
    o_ref[...] = y.astype(BF16)
    ot_ref[...] = y.T.astype(BF16)


def _cmp(layer, kvc, w1, pos, b1, w2, b2, rc, rs1, rs2):
    _, B, T, _ = kvc.shape
    nr = T // CMP_STRIDE
    rw = CMP_STRIDE * KV_W
    per_kv = lambda shape: pl.BlockSpec((None, None) + shape, lambda s, b: (layer, s) + (0,) * len(shape))
    tab = pl.BlockSpec((None, nr, LANES), lambda s, b: (s, 0, 0))
    return pl.pallas_call(
        _cmp_kernel,
        grid=(2, B),
        in_specs=[pl.BlockSpec((None, None, T, KV_W), lambda s, b: (s, b, 0, 0)),
                  per_kv((2, rw, 2 * CMP_HIDDEN)), per_kv((CMP_BLOCK, KV_W)), per_kv((1, 2 * CMP_HIDDEN)),
                  per_kv((2 * CMP_HIDDEN, KV_W)), per_kv((1, KV_W)), tab, tab, tab],
        out_specs=[pl.BlockSpec((None, None, nr, KV_W), lambda s, b: (s, b, 0, 0)),
                   pl.BlockSpec((None, None, KV_W, nr), lambda s, b: (s, b, 0, 0))],
        out_shape=[jax.ShapeDtypeStruct((2, B, nr, KV_W), BF16), jax.ShapeDtypeStruct((2, B, KV_W, nr), BF16)],
        compiler_params=_cparams(("arbitrary", "arbitrary")),
        name="cmp",
    )(kvc, w1, pos, b1, w2, b2, rc, rs1, rs2)


def _group_query_weights(qt, g):
    zero = jnp.zeros((HEAD_DIM, qt.shape[1]), qt.dtype)
    out = []
    for r in range(NSA_GROUP):
        q = qt[r * HEAD_DIM:(r + 1) * HEAD_DIM, :]
        out.append(jnp.concatenate([jnp.where(g == 0, q, zero), jnp.where(g == 1, q, zero)], axis=0))
    return out


def _cmpsel_kernel(qt_ref, kc_ref, vct_ref, mapt_ref, oc_ref, mt_ref, *, tq, n_sel, top):
    g = pl.program_id(1)
    i = pl.program_id(2)
    nr = kc_ref.shape[2]
    kc = kc_ref[0, 0]
    vct = vct_ref[0, 0]
    cend = lax.broadcasted_iota(jnp.int32, (nr, tq), 0) * CMP_STRIDE + (CMP_BLOCK - 1)
    qpos = i * tq + lax.broadcasted_iota(jnp.int32, (nr, tq), 1)
    valid = cend <= qpos
    sts = [_dot(kc, w) for w in _group_query_weights(qt_ref[0], g)]
    vct_ones = _with_ones_rows(vct)
    imp = jnp.zeros((LANES, tq), F32)
    outs = []
    for st in sts:
        st = jnp.where(valid, st, NEG_BIG)
        e = jnp.where(valid, jnp.exp2(st - jnp.max(st, axis=0, keepdims=True)), 0.0).astype(BF16)
        oe = _dot(vct_ones, e)
        den = oe[HEAD_DIM:HEAD_DIM + 1, :]
        inv = jnp.where(den > 0.0, 1.0 / jnp.where(den > 0.0, den, 1.0), 0.0)
        outs.append(oe[0:HEAD_DIM, :] * inv)
        imp = imp + _dot(mapt_ref[...], e) * inv
    oc_ref[0] = jnp.concatenate(outs, axis=0).T

    sid = lax.broadcasted_iota(jnp.int32, (LANES, tq), 0)
    qp = i * tq + lax.broadcasted_iota(jnp.int32, (LANES, tq), 1)
    score0 = jnp.where((sid * SEL_BLOCK > qp) | (sid >= n_sel), -3.0,
                       jnp.where(sid == (qp >> 6), FORCE_SCORE, jnp.where(sid == 0, 2.0 * FORCE_SCORE, imp)))

    score = score0
    for _ in range(top):
        score = jnp.where(score == jnp.max(score, axis=0, keepdims=True), -2.0, score)
    picked = score == -2.0
    mt_ref[0, 0] = jnp.where(picked, 0.0, NEG_BIG).astype(BF16)
    n_picked = jnp.sum(jnp.where(picked, 1.0, 0.0), axis=0, keepdims=True)

    @pl.when(jnp.max(n_picked) > top)
    def _():
        sidf = sid.astype(F32)
        score = score0
        maskt = jnp.full((LANES, tq), NEG_BIG, F32)
        for _ in range(top):
            mx = jnp.max(score, axis=0, keepdims=True)
            idx = jnp.min(jnp.where(score == mx, sidf, float(LANES)), axis=0, keepdims=True)
            pick = (sidf == idx) & (mx > -3.0)
            maskt = jnp.where(pick, 0.0, maskt)
            score = jnp.where(pick, -2.0, score)
        mt_ref[0, 0] = maskt.astype(BF16)


def _cmpsel(qbt, kc, vct, selmap_t):
    B, _, T = qbt.shape
    nr = kc.shape[2]
    tq = NSA_TQ
    n_sel = T // SEL_BLOCK
    return pl.pallas_call(
        functools.partial(_cmpsel_kernel, tq=tq, n_sel=n_sel, top=min(SEL_TOPK, n_sel)),
        grid=(B, NSA_KV_HEADS, T // tq),
        in_specs=[pl.BlockSpec((1, GROUP_W, tq), lambda b, g, i: (b, g, i)),
                  pl.BlockSpec((1, 1, nr, KV_W), lambda b, g, i: (0, b, 0, 0)),
                  pl.BlockSpec((1, 1, HEAD_DIM, nr), lambda b, g, i: (1, b, g, 0)),
                  pl.BlockSpec((LANES, nr), lambda b, g, i: (0, 0))],
        out_specs=[pl.BlockSpec((1, tq, GROUP_W), lambda b, g, i: (b, i, g)),
                   pl.BlockSpec((1, 1, LANES, tq), lambda b, g, i: (b, g, 0, i))],
        out_shape=[jax.ShapeDtypeStruct((B, T, NSA_W), F32),
                   jax.ShapeDtypeStruct((B, NSA_KV_HEADS, LANES, T), BF16)],
        compiler_params=_cparams(("arbitrary", "arbitrary", "arbitrary")),
        name="cmpsel",
    )(qbt, kc, vct, selmap_t)


def _sel_kernel(qt_ref, mt_ref, k_ref, vt_ref, kw_ref, vwt_ref, os_ref, ow_ref, w_scr, *scr, tq, tk):
    g = pl.program_id(1)
    for r, w in enumerate(_group_query_weights(qt_ref[0], g)):
        w_scr[r] = jnp.concatenate([w, mt_ref[0, 0]], axis=0)

    def score_fn(r, start):
        return _dot(k_ref[0, pl.ds(start, tk), :], w_scr[r])

    def vt_fn(r, start):
        return vt_ref[0, :, pl.ds(start, tk)]

    def w_score_fn(r, start):
        return _dot(kw_ref[0, pl.ds(start, tk), :], w_scr[r, 0:KV_W, :])

    def w_vt_fn(r, start):
        return vwt_ref[0, :, pl.ds(start, tk)]

    ost, owt = _causal_flash_t(pl.program_id(2), NSA_GROUP, tq, tk, score_fn, vt_fn, scr,
                               window=(w_score_fn, w_vt_fn))
    os_ref[0] = ost.T
    ow_ref[0] = owt.T


def _sel(qbt, maskt, ks, vst, kw, vwt):
    B, _, T = qbt.shape
    tq, tk = min(SEL_TQ, T), min(SEL_TK, T)
    kaug = KV_W + LANES
    out = pl.BlockSpec((1, tq, GROUP_W), lambda b, g, i: (b, i, g))
    return pl.pallas_call(
        functools.partial(_sel_kernel, tq=tq, tk=tk),
        grid=(B, NSA_KV_HEADS, T // tq),
        in_specs=[pl.BlockSpec((1, GROUP_W, tq), lambda b, g, i: (b, g, i)),
                  pl.BlockSpec((1, 1, LANES, tq), lambda b, g, i: (b, g, 0, i)),
                  pl.BlockSpec((1, T, kaug), lambda b, g, i: (b, 0, 0)),
                  pl.BlockSpec((1, HEAD_DIM, T), lambda b, g, i: (b, g, 0)),
                  pl.BlockSpec((1, T, KV_W), lambda b, g, i: (b, 0, 0)),
                  pl.BlockSpec((1, HEAD_DIM, T), lambda b, g, i: (b, g, 0))],
        out_specs=[out, out],
        out_shape=[jax.ShapeDtypeStruct((B, T, NSA_W), F32)] * 2,
        scratch_shapes=[pltpu.VMEM((NSA_GROUP, kaug, tq), BF16)] + _flash_scratch(NSA_GROUP, tq, tk),
        compiler_params=_cparams(("arbitrary", "arbitrary", "arbitrary")),
        name="sel",
    )(qbt, maskt, ks, vst, kw, vwt)


def _merge_kernel(x_ref, g_ref, oa_ref, oc_ref, os_ref, ow_ref, gb_ref, wg_ref, wof_ref, won_ref, wout_ref, e_ref,
                  o_ref):
    x = x_ref[...]
    h = _rmsnorm(x, g_ref[...]).astype(BF16)
    ya = _dot(oa_ref[...], wof_ref[...])
    sg = _pack3(_sigmoid(gb_ref[...]), 3 * NSA_HEADS)
    on = (_dot(sg, e_ref[0]) * oc_ref[...] + _dot(sg, e_ref[1]) * os_ref[...]
          + _dot(sg, e_ref[2]) * ow_ref[...])
    yb = _dot(on.astype(BF16), won_ref[...])
    mixed = (_sigmoid(_dot(h, wg_ref[:, 0:D_MODEL])) * ya
             + _sigmoid(_dot(h, wg_ref[:, D_MODEL:2 * D_MODEL])) * yb)
    o_ref[...] = x + _dot(mixed.astype(BF16), wout_ref[...])


def _layer_spec(layer, shape):
    return pl.BlockSpec((None,) + shape, lambda *_: (layer,) + (0,) * len(shape))


def _merge(layer, x, g, oa, oc, os_, ow, gb, wp, wof, won, wout, e):
    n = x.shape[0]
    tm = MERGE_TM
    row = lambda w: pl.BlockSpec((tm, w), lambda t: (t, 0))
    const = lambda shape: pl.BlockSpec(shape, lambda t: (0,) * len(shape))
    assert OFF_GATE % (2 * D_MODEL) == 0
    gate_w = pl.BlockSpec((None, D_MODEL, 2 * D_MODEL), lambda t: (layer, 0, OFF_GATE // (2 * D_MODEL)))
    return pl.pallas_call(
        _merge_kernel,
        grid=(n // tm,),
        in_specs=[row(D_MODEL), _layer_spec(layer, (1, D_MODEL)), row(FOX_W), row(NSA_W), row(NSA_W), row(NSA_W),
                  row(LANES), gate_w,
                  _layer_spec(layer, (FOX_W, D_MODEL)), _layer_spec(layer, (NSA_W, D_MODEL)),
                  _layer_spec(layer, (D_MODEL, D_MODEL)), const((3, LANES, NSA_W))],
        out_specs=row(D_MODEL),
        out_shape=jax.ShapeDtypeStruct((n, D_MODEL), F32),
        compiler_params=_cparams(("arbitrary",)),
        name="merge",
    )(x, g, oa, oc, os_, ow, gb, wp, wof, won, wout, e)


def _mlp_kernel(x_ref, g_ref, wu_ref, wd_ref, gf_ref, o_ref, h_scr, *, final_norm):
    x = x_ref[...]
    h_scr[...] = _rmsnorm(x, g_ref[...]).astype(BF16)
    acc = None
    for c0 in range(0, D_FF, MLP_FF_CHUNK):
        u = jnp.maximum(_dot(h_scr[...], wu_ref[:, c0:c0 + MLP_FF_CHUNK]), 0.0)
        part = _dot((u * u).astype(BF16), wd_ref[c0:c0 + MLP_FF_CHUNK, :])
        acc = part if acc is None else acc + part
    y = x + acc
    o_ref[...] = _rmsnorm(y, gf_ref[...]) if final_norm else y


def _mlp(layer, x, g, wu, wd, gf, final_norm):
    n = x.shape[0]
    tm = MLP_TM
    row = pl.BlockSpec((tm, D_MODEL), lambda t: (t, 0))
    const = lambda shape: pl.BlockSpec(shape, lambda t: (0,) * len(shape))
    return pl.pallas_call(
        functools.partial(_mlp_kernel, final_norm=final_norm),
        grid=(n // tm,),
        in_specs=[row, _layer_spec(layer, (1, D_MODEL)), _layer_spec(layer, (D_MODEL, D_FF)),
                  _layer_spec(layer, (D_FF, D_MODEL)), const((1, D_MODEL))],
        out_specs=row,
        out_shape=jax.ShapeDtypeStruct((n, D_MODEL), F32),
        scratch_shapes=[pltpu.VMEM((tm, D_MODEL), BF16)],
        compiler_params=_cparams(("arbitrary",)),
        name="mlp",
    )(x, g, wu, wd, gf)


def _rope_tables(pos):
    inv = ROPE_THETA ** (-jnp.arange(ROPE_HALF, dtype=F32) / ROPE_HALF)
    ang = pos.astype(F32)[:, None] * inv[None, :]
    cos, sin = jnp.cos(ang), jnp.sin(ang)
    n = pos.shape[0]
    rest = HEAD_DIM - ROPE_DIM
    rc = jnp.concatenate([cos, cos, jnp.ones((n, rest), F32)], axis=1)
    rs1 = jnp.concatenate([-sin, jnp.zeros((n, HEAD_DIM - ROPE_HALF), F32)], axis=1)
    rs2 = jnp.concatenate([jnp.zeros((n, ROPE_HALF), F32), sin, jnp.zeros((n, rest), F32)], axis=1)
    two = lambda t: jnp.concatenate([t, t], axis=1)
    return two(rc), two(rs1), two(rs2)


def _pad_heads(w, n_heads):
    zero = jnp.zeros(w.shape[:-1] + (LANES - HEAD_DIM,), w.dtype)
    parts = []
    for h in range(n_heads):
        parts += [w[..., h * HEAD_DIM:(h + 1) * HEAD_DIM], zero]
    return jnp.concatenate(parts, axis=-1)


def _pad_cols(w, width):
    return jnp.concatenate([w, jnp.zeros(w.shape[:-1] + (width - w.shape[-1],), w.dtype)], axis=-1)


def _proj_weight(w_in):
    o = np.cumsum([0, FOX_W, FOX_W, FOX_W, FOX_HEADS, NSA_W, 6 * KV_W, 3 * NSA_HEADS, D_MODEL, D_MODEL])
    cols = lambda j: w_in[..., o[j]:o[j + 1]]
    qscale = SCALE * LOG2E
    parts = [
        _pad_heads(cols(0) * qscale, FOX_HEADS), _pad_heads(cols(1), FOX_HEADS), cols(2),
        _pad_cols(cols(3), LANES),
        cols(4) * qscale,
        cols(5),
        _pad_cols(cols(6), LANES),
        cols(7), cols(8),
    ]
    return jnp.concatenate(parts, axis=-1).astype(BF16)


def _aug_placement():
    pq = np.zeros((LANES, FOX_PAD_W), np.float32)
    pk = np.zeros((LANES, FOX_PAD_W), np.float32)
    for h in range(FOX_HEADS):
        for piece in range(3):
            pq[piece * PIECE_LANES + h, h * LANES + AUG_Q_C + piece] = 1.0
            pk[piece * PIECE_LANES + h, h * LANES + AUG_K_C + piece] = -1.0
    return jnp.asarray(pq, BF16), jnp.asarray(pk, BF16)


def _gate_placement():
    e = np.zeros((3, LANES, NSA_W), np.float32)
    for j in range(3):
        for h in range(NSA_HEADS):
            for piece in range(3):
                e[j, piece * PIECE_LANES + 3 * h + j, h * HEAD_DIM:(h + 1) * HEAD_DIM] = 1.0
    return jnp.asarray(e, BF16)


def _sel_map_t(T, nr):
    nc = (T - CMP_BLOCK) // CMP_STRIDE + 1
    n_sel = T // SEL_BLOCK
    cs = np.arange(nc) * CMP_STRIDE
    ce = cs + CMP_BLOCK
    ss = np.arange(n_sel) * SEL_BLOCK
    se = ss + SEL_BLOCK
    ov = np.clip(np.minimum(ce[None, :], se[:, None]) - np.maximum(cs[None, :], ss[:, None]), 0, None)
    m = np.zeros((LANES, nr), np.float32)
    m[:n_sel, :nc] = ov / CMP_BLOCK
    return jnp.asarray(m, BF16)


def _cmp_weights(cmp_pos, cmp_w1, cmp_b1, cmp_w2, cmp_b2):
    d, half, g2 = cmp_w1.shape[0], CMP_BLOCK // 2, NSA_KV_HEADS

    def group_diag(w, axis):
        z = jnp.zeros_like(w)
        return jnp.stack([jnp.concatenate([w, z], axis=-1), jnp.concatenate([z, w], axis=-1)], axis=axis)

    w1 = group_diag(cmp_w1.astype(BF16).reshape(d, 2, 2, half, HEAD_DIM, CMP_HIDDEN), 4)
    w1 = w1.reshape(d, 2, 2, half * KV_W, g2 * CMP_HIDDEN)
    pos = jnp.concatenate([cmp_pos] * g2, axis=-1)
    b1 = jnp.concatenate([cmp_b1] * g2, axis=-1)[:, :, None, :]
    w2 = group_diag(cmp_w2.astype(BF16), 2).reshape(d, 2, g2 * CMP_HIDDEN, KV_W)
    b2 = jnp.concatenate([cmp_b2] * g2, axis=-1)[:, :, None, :]
    return w1, pos, b1, w2, b2


def kernel(x, norm_mix, w_in, b_forget, cmp_pos, cmp_w1, cmp_b1, cmp_w2, cmp_b2, w_o_fox, w_o_nsa, w_out,
           norm_mlp, w_up, w_down, norm_final):
    B, T, _ = x.shape
    depth = w_in.shape[0]
    nr = T // CMP_STRIDE
    n = B * T
    flat = lambda a: a.reshape(n, a.shape[-1])

    pq, pk = _aug_placement()
    gate_e = _gate_placement()
    selmap_t = _sel_map_t(T, nr)
    rope = _rope_tables(jnp.arange(T))
    crope = [jnp.stack([t, jnp.full_like(t, fill)]) for t, fill in
             zip(_rope_tables(jnp.arange(nr) * CMP_STRIDE + CMP_BLOCK - 1), (1.0, 0.0, 0.0))]
    wp = _proj_weight(w_in)
    bf = _pad_cols(b_forget, LANES)[:, None, :]
    cmp_w = _cmp_weights(cmp_pos, cmp_w1, cmp_b1, cmp_w2, cmp_b2)
    wof, won, wout = w_o_fox.astype(BF16), w_o_nsa.astype(BF16), w_out.astype(BF16)
    wu, wd = w_up.astype(BF16), w_down.astype(BF16)
    g_mix, g_mlp = norm_mix[:, None, :], norm_mlp[:, None, :]

    for l in range(depth):
        qat, ka, vat, qbt, ks, kw, vst, vwt, kvc, gb = _proj(l, x, g_mix, wp, bf, pq, pk, *rope)
        oa = _fox(qat, ka, vat)
        cmp_rows, cmp_cols = _cmp(l, kvc, *cmp_w, *crope)
        oc, maskt = _cmpsel(qbt, cmp_rows, cmp_cols, selmap_t)
        os_, ow = _sel(qbt, maskt, ks, vst, kw, vwt)
        x1 = _merge(l, flat(x), g_mix, flat(oa), flat(oc), flat(os_), flat(ow), flat(gb), wp,
                    wof, won, wout, gate_e)
        x = _mlp(l, x1, g_mlp, wu, wd, norm_final[None, :], l == depth - 1).reshape(B, T, D_MODEL)
    return x
```

```python
import functools

import numpy as np
import jax
import jax.numpy as jnp
from jax import lax
from jax.experimental import pallas as pl
from jax.experimental.pallas import tpu as pltpu

F32 = jnp.float32
BF16 = jnp.bfloat16

D_MODEL = 1024
HEAD_DIM = 64
FOX_HEADS = 8
NSA_HEADS = 8
NSA_KV_HEADS = 2
NSA_GROUP = NSA_HEADS // NSA_KV_HEADS
CMP_BLOCK = 32
CMP_STRIDE = 16
CMP_HIDDEN = 4 * HEAD_DIM
SEL_BLOCK = 64
SEL_TOPK = 16
WINDOW = 512
ROPE_THETA = 500000.0
ROPE_DIM = HEAD_DIM // 4
ROPE_HALF = ROPE_DIM // 2
D_FF = 4 * D_MODEL
RMS_EPS = 1e-6
FORCE_SCORE = 1e6
NEG_BIG = -1e30
SCALE = HEAD_DIM ** -0.5
LOG2E = 1.4426950408889634

FOX_W = FOX_HEADS * HEAD_DIM
NSA_W = NSA_HEADS * HEAD_DIM
KV_W = NSA_KV_HEADS * HEAD_DIM

LANES = 128
DEN_ROWS = 16
PIECE_LANES = 32
VMEM_LIMIT_BYTES = 56 * 1024 * 1024

FOX_PAD_W = FOX_HEADS * LANES
GROUP_W = NSA_GROUP * HEAD_DIM
OFF_QA = 0
OFF_KA = OFF_QA + FOX_PAD_W
OFF_VA = OFF_KA + FOX_PAD_W
OFF_F = OFF_VA + FOX_W
OFF_QB = OFF_F + LANES
OFF_KC = OFF_QB + NSA_W
OFF_VC = OFF_KC + KV_W
OFF_KS = OFF_VC + KV_W
OFF_VS = OFF_KS + KV_W
OFF_KW = OFF_VS + KV_W
OFF_VW = OFF_KW + KV_W
OFF_GB = OFF_VW + KV_W
OFF_GATE = OFF_GB + LANES
PROJ_W = OFF_GATE + 2 * D_MODEL

AUG_Q_C = HEAD_DIM
AUG_K_C = HEAD_DIM + 3

PROJ_TM = 512
FOX_TQ = 512
FOX_TK = 512
FOX_HEADS_PER_STEP = 4
NSA_TQ = 512
SEL_TQ = 512
SEL_TK = 512
MERGE_TM = 512
MLP_TM = 512
MLP_FF_CHUNK = 512


def _cparams(sem):
    return pltpu.CompilerParams(dimension_semantics=sem, vmem_limit_bytes=VMEM_LIMIT_BYTES)


def _dot(a, b):
    return jnp.dot(a, b, preferred_element_type=F32)


def _split3(x):
    hi = x.astype(BF16)
    r = x - hi.astype(F32)
    mid = r.astype(BF16)
    lo = (r - mid.astype(F32)).astype(BF16)
    return hi, mid, lo


def _pack3(x, n_real):
    lane = lax.broadcasted_iota(jnp.int32, x.shape, 1)
    hi, mid, lo = _split3(jnp.where(lane < n_real, x, 0.0))
    return (hi.astype(F32) + pltpu.roll(mid.astype(F32), PIECE_LANES, 1)
            + pltpu.roll(lo.astype(F32), 2 * PIECE_LANES, 1)).astype(BF16)


def _lane_tile(x, reps):
    return x if reps == 1 else jnp.concatenate([x] * reps, axis=1)


def _rope(y, rc, rs1, rs2):
    n = y.shape[1]
    return y * rc + pltpu.roll(y, n - ROPE_HALF, 1) * rs1 + pltpu.roll(y, ROPE_HALF, 1) * rs2


def _rmsnorm(x, g):
    return x * lax.rsqrt(jnp.mean(x * x, axis=-1, keepdims=True) + RMS_EPS) * g


def _sigmoid(x):
    return 1.0 / (1.0 + jnp.exp(-x))


def _proj_kernel(x_ref, g_ref, w_ref, bf_ref, pq_ref, pk_ref, rc_ref, rs1_ref, rs2_ref,
                 qa_ref, ka_ref, va_ref, qb_ref, ks_ref, kw_ref, vs_ref, vw_ref, kvc_ref, gb_ref,
                 h_scr, carry_scr, *, tm):
    t = pl.program_id(1)

    @pl.when(t == 0)
    def _():
        carry_scr[...] = jnp.zeros_like(carry_scr)

    h_scr[...] = _rmsnorm(x_ref[0], g_ref[...]).astype(BF16)

    def seg(c0, n):
        return _dot(h_scr[...], w_ref[:, c0:c0 + n])

    f = seg(OFF_F, LANES) + bf_ref[...]
    lf = jnp.minimum(f, 0.0) - jnp.log1p(jnp.exp(-jnp.abs(f)))
    row = lax.broadcasted_iota(jnp.int32, (tm, tm), 0)
    col = lax.broadcasted_iota(jnp.int32, (tm, tm), 1)
    tri = jnp.where(col <= row, 1.0, 0.0).astype(BF16)
    hi, mid, lo = _split3(lf)
    c = _dot(tri, hi) + _dot(tri, mid) + _dot(tri, lo) + carry_scr[0:1, :]
    carry_scr[0:1, :] = c[tm - 1:tm, :]
    c3 = _pack3(c * LOG2E, FOX_HEADS)

    lane = lax.broadcasted_iota(jnp.int32, (tm, FOX_PAD_W), 1) & (LANES - 1)
    ones_q = jnp.where((lane >= AUG_K_C) & (lane < AUG_K_C + 3), 1.0, 0.0)
    ones_k = jnp.where((lane >= AUG_Q_C) & (lane < AUG_Q_C + 3), 1.0, 0.0)
    qa_ref[0] = (seg(OFF_QA, FOX_PAD_W) + _dot(c3, pq_ref[...]) + ones_q).T.astype(BF16)
    ka_ref[0] = (seg(OFF_KA, FOX_PAD_W) + _dot(c3, pk_ref[...]) + ones_k).astype(BF16)
    va_ref[0] = seg(OFF_VA, FOX_W).T.astype(BF16)

    rc, rs1, rs2 = rc_ref[...], rs1_ref[...], rs2_ref[...]
    reps = NSA_W // LANES
    qb_ref[0] = _rope(seg(OFF_QB, NSA_W), _lane_tile(rc, reps), _lane_tile(rs1, reps),
                      _lane_tile(rs2, reps)).T.astype(BF16)
    pos = t * tm + lax.broadcasted_iota(jnp.int32, (tm, LANES), 0)
    onehot = jnp.where((pos >> 6) == lax.broadcasted_iota(jnp.int32, (tm, LANES), 1), 1.0, 0.0)
    ks_ref[0] = jnp.concatenate([_rope(seg(OFF_KS, KV_W), rc, rs1, rs2), onehot], axis=1).astype(BF16)
    kw_ref[0] = _rope(seg(OFF_KW, KV_W), rc, rs1, rs2).astype(BF16)
    vs_ref[0] = seg(OFF_VS, KV_W).T.astype(BF16)
    vw_ref[0] = seg(OFF_VW, KV_W).T.astype(BF16)
    kvc_ref[0, 0] = seg(OFF_KC, KV_W)
    kvc_ref[1, 0] = seg(OFF_VC, KV_W)
    gb_ref[0] = seg(OFF_GB, LANES)


def _proj(layer, x, g, wp, bf, pq, pk, rc, rs1, rs2):
    B, T, _ = x.shape
    tm = PROJ_TM
    row = lambda w: pl.BlockSpec((1, tm, w), lambda b, t: (b, t, 0))
    col = lambda w: pl.BlockSpec((1, w, tm), lambda b, t: (b, 0, t))
    const = lambda shape: pl.BlockSpec(shape, lambda b, t: (0,) * len(shape))
    tab = pl.BlockSpec((tm, LANES), lambda b, t: (t, 0))
    out_shapes = [
        jax.ShapeDtypeStruct((B, FOX_PAD_W, T), BF16),
        jax.ShapeDtypeStruct((B, T, FOX_PAD_W), BF16),
        jax.ShapeDtypeStruct((B, FOX_W, T), BF16),
        jax.ShapeDtypeStruct((B, NSA_W, T), BF16),
        jax.ShapeDtypeStruct((B, T, KV_W + LANES), BF16),
        jax.ShapeDtypeStruct((B, T, KV_W), BF16),
        jax.ShapeDtypeStruct((B, KV_W, T), BF16),
        jax.ShapeDtypeStruct((B, KV_W, T), BF16),
        jax.ShapeDtypeStruct((2, B, T, KV_W), F32),
        jax.ShapeDtypeStruct((B, T, LANES), F32),
    ]
    out_specs = [
        col(FOX_PAD_W), row(FOX_PAD_W), col(FOX_W), col(NSA_W),
        row(KV_W + LANES), row(KV_W), col(KV_W), col(KV_W),
        pl.BlockSpec((2, 1, tm, KV_W), lambda b, t: (0, b, t, 0)),
        row(LANES),
    ]
    return pl.pallas_call(
        functools.partial(_proj_kernel, tm=tm),
        grid=(B, T // tm),
        in_specs=[row(D_MODEL), _layer_spec(layer, (1, D_MODEL)), _layer_spec(layer, (D_MODEL, OFF_GATE)),
                  _layer_spec(layer, (1, LANES)),
                  const((LANES, FOX_PAD_W)), const((LANES, FOX_PAD_W)), tab, tab, tab],
        out_specs=out_specs,
        out_shape=out_shapes,
        scratch_shapes=[pltpu.VMEM((tm, D_MODEL), BF16), pltpu.VMEM((8, LANES), F32)],
        compiler_params=_cparams(("arbitrary", "arbitrary")),
        name="proj",
    )(x, g, wp, bf, pq, pk, rc, rs1, rs2)


def _with_ones_rows(vt):
    first = lax.broadcasted_iota(jnp.int32, (DEN_ROWS, vt.shape[1]), 0) == 0
    return jnp.concatenate([vt, jnp.where(first, 1.0, 0.0).astype(vt.dtype)], axis=0)


def _causal_flash_t(i, n_heads, tq, tk, score_fn, vt_fn, scr, window=None):
    s0_scr, s1_scr, m_scr, acc_scr = scr
    n_full = (i * tq) // tk

    def scores(j, s_scr, fn=score_fn):
        start = pl.multiple_of(j * tk, tk)
        for h in range(n_heads):
            s_scr[h] = fn(h, start)

    def softmax_pv(j, s_scr):
        start = pl.multiple_of(j * tk, tk)
        for h in range(n_heads):
            st = s_scr[h]
            m = m_scr[h]
            m_new = jnp.maximum(m, jnp.max(st, axis=0, keepdims=True))
            p = jnp.exp2(st - m_new).astype(BF16)
            acc_scr[h] = jnp.exp2(m - m_new) * acc_scr[h] + _dot(_with_ones_rows(vt_fn(h, start)), p)
            m_scr[h] = m_new

    def last_tile():
        half = tk // 2
        start = pl.multiple_of(n_full * tk, tk)
        tri = (lax.broadcasted_iota(jnp.int32, (half, tq), 0) <= lax.broadcasted_iota(jnp.int32, (half, tq), 1))
        neg = jnp.full((1, half), NEG_BIG, F32)
        pad = jnp.zeros((HEAD_DIM + DEN_ROWS, half), F32)
        for h in range(n_heads):
            top = jnp.where(tri, s0_scr[h, 0:half, :], NEG_BIG)
            bot = jnp.where(tri[:, 0:half], s0_scr[h, half:tk, half:tq], NEG_BIG)
            m = m_scr[h]
            m_bot = jnp.concatenate([neg, jnp.max(bot, axis=0, keepdims=True)], axis=1)
            m_new = jnp.maximum(jnp.maximum(m, jnp.max(top, axis=0, keepdims=True)), m_bot)
            vt = _with_ones_rows(vt_fn(h, start))
            upd = _dot(vt[:, 0:half], jnp.exp2(top - m_new).astype(BF16))
            upd_right = _dot(vt[:, half:tk], jnp.exp2(bot - m_new[:, half:tq]).astype(BF16))
            acc_scr[h] = (jnp.exp2(m - m_new) * acc_scr[h] + upd + jnp.concatenate([pad, upd_right], axis=1))
            m_scr[h] = m_new

    def normalised(acc_ref):
        return jnp.concatenate([acc_ref[h, 0:HEAD_DIM, :] / acc_ref[h, HEAD_DIM:HEAD_DIM + 1, :]
                                for h in range(n_heads)], axis=0)

    assert tq == tk
    m_scr[...] = jnp.full_like(m_scr, NEG_BIG)
    acc_scr[...] = jnp.zeros_like(acc_scr)

    odd = n_full & 1

    @pl.when(odd == 1)
    def _():
        scores(0, s1_scr)
        scores(1, s0_scr)
        softmax_pv(0, s1_scr)

    @pl.when(odd == 0)
    def _():
        scores(0, s0_scr)

    def pair(t0):
        scores(t0 + 1, s1_scr)
        softmax_pv(t0, s0_scr)
        scores(t0 + 2, s0_scr)
        softmax_pv(t0 + 1, s1_scr)

    n_pairs = n_full // 2
    odd_pair = n_pairs & 1

    @pl.when(odd_pair == 1)
    def _():
        pair(odd)

    def body(jq, carry):
        t0 = odd + 2 * odd_pair + 4 * jq
        pair(t0)
        pair(t0 + 2)
        return carry

    lax.fori_loop(0, n_pairs // 2, body, 0)
    if window is None:
        last_tile()
        return normalised(acc_scr), None

    assert tq == tk == WINDOW
    w_score, w_vt = window
    prev = jnp.maximum(i - 1, 0)
    first_row = jnp.where(i > 0, 0, tk + tq)
    scores(i, s1_scr, w_score)
    last_tile()
    scores(prev, s0_scr, w_score)
    rel = (lax.broadcasted_iota(jnp.int32, (tk, tq), 0) - lax.broadcasted_iota(jnp.int32, (tk, tq), 1))
    cur, start, start_prev = rel <= 0, pl.multiple_of(i * tk, tk), pl.multiple_of(prev * tk, tk)
    outs = []
    for h in range(n_heads):
        st = jnp.where(cur, s1_scr[h], jnp.where(rel > first_row, s0_scr[h], NEG_BIG))
        e = jnp.exp2(st - jnp.max(st, axis=0, keepdims=True))
        oe = (_dot(_with_ones_rows(w_vt(h, start)), jnp.where(cur, e, 0.0).astype(BF16))
              + _dot(_with_ones_rows(w_vt(h, start_prev)), jnp.where(cur, 0.0, e).astype(BF16)))
        outs.append(oe[0:HEAD_DIM, :] / oe[HEAD_DIM:HEAD_DIM + 1, :])
    return normalised(acc_scr), jnp.concatenate(outs, axis=0)


def _flash_scratch(n_heads, tq, tk):
    return [pltpu.VMEM((n_heads, tk, tq), F32), pltpu.VMEM((n_heads, tk, tq), F32),
            pltpu.VMEM((n_heads, 1, tq), F32), pltpu.VMEM((n_heads, HEAD_DIM + DEN_ROWS, tq), F32)]


def _fox_kernel(qt_ref, k_ref, vt_ref, o_ref, *scr, tq, tk, heads):
    def score_fn(h, start):
        return _dot(k_ref[0, pl.ds(start, tk), h * LANES:(h + 1) * LANES], qt_ref[0, h * LANES:(h + 1) * LANES, :])

    def vt_fn(h, start):
        return vt_ref[0, h * HEAD_DIM:(h + 1) * HEAD_DIM, pl.ds(start, tk)]

    ot, _ = _causal_flash_t(pl.program_id(2), heads, tq, tk, score_fn, vt_fn, scr)
    o_ref[0] = ot.T.astype(BF16)


def _fox(qat, ka, vat):
    B, T, _ = ka.shape
    tq, tk = min(FOX_TQ, T), min(FOX_TK, T)
    heads = FOX_HEADS_PER_STEP
    return pl.pallas_call(
        functools.partial(_fox_kernel, tq=tq, tk=tk, heads=heads),
        grid=(B, FOX_HEADS // heads, T // tq),
        in_specs=[pl.BlockSpec((1, heads * LANES, tq), lambda b, h, i: (b, h, i)),
                  pl.BlockSpec((1, T, heads * LANES), lambda b, h, i: (b, 0, h)),
                  pl.BlockSpec((1, heads * HEAD_DIM, T), lambda b, h, i: (b, h, 0))],
        out_specs=pl.BlockSpec((1, tq, heads * HEAD_DIM), lambda b, h, i: (b, i, h)),
        out_shape=jax.ShapeDtypeStruct((B, T, FOX_W), BF16),
        scratch_shapes=_flash_scratch(heads, tq, tk),
        compiler_params=_cparams(("arbitrary", "arbitrary", "arbitrary")),
        name="fox",
    )(qat, ka, vat)


def _cmp_kernel(r_ref, w1_ref, pos_ref, b1_ref, w2_ref, b2_ref, rc_ref, rs1_ref, rs2_ref, o_ref, ot_ref):
    nr = o_ref.shape[0]
    half = CMP_BLOCK // 2
    a = bm = None
    for l in range(half):
        x = r_ref[pl.ds(l, nr, stride=CMP_STRIDE), :]
        ta = _dot((x + pos_ref[l:l + 1, :]).astype(BF16), w1_ref[0, l * KV_W:(l + 1) * KV_W, :])
        tb = _dot((x + pos_ref[half + l:half + l + 1, :]).astype(BF16), w1_ref[1, l * KV_W:(l + 1) * KV_W, :])
        a = ta if a is None else a + ta
        bm = tb if bm is None else bm + tb
    hdn = a + pltpu.roll(bm, nr - 1, 0) + b1_ref[...]
    act = 0.5 * hdn * (1.0 + jnp.tanh(0.7978845608028654 * (hdn + 0.044715 * (hdn * hdn * hdn))))
    y = _rope(_dot(act.astype(BF16), w2_ref[...]) + b2_ref[...], rc_ref[...], rs1_ref[...], rs2_ref[...])
    o_ref[...] = y.astype(BF16)
    ot_ref[...] = y.T.astype(BF16)


def _cmp(layer, kvc, w1, pos, b1, w2, b2, rc, rs1, rs2):
    _, B, T, _ = kvc.shape
    nr = T // CMP_STRIDE
    rw = CMP_STRIDE * KV_W
    per_kv = lambda shape: pl.BlockSpec((None, None) + shape, lambda s, b: (layer, s) + (0,) * len(shape))
    tab = pl.BlockSpec((None, nr, LANES), lambda s, b: (s, 0, 0))
    return pl.pallas_call(
        _cmp_kernel,
        grid=(2, B),
        in_specs=[pl.BlockSpec((None, None, T, KV_W), lambda s, b: (s, b, 0, 0)),
                  per_kv((2, rw, 2 * CMP_HIDDEN)), per_kv((CMP_BLOCK, KV_W)), per_kv((1, 2 * CMP_HIDDEN)),
                  per_kv((2 * CMP_HIDDEN, KV_W)), per_kv((1, KV_W)), tab, tab, tab],
        out_specs=[pl.BlockSpec((None, None, nr, KV_W), lambda s, b: (s, b, 0, 0)),
                   pl.BlockSpec((None, None, KV_W, nr), lambda s, b: (s, b, 0, 0))],
        out_shape=[jax.ShapeDtypeStruct((2, B, nr, KV_W), BF16), jax.ShapeDtypeStruct((2, B, KV_W, nr), BF16)],
        compiler_params=_cparams(("arbitrary", "arbitrary")),
        name="cmp",
    )(kvc, w1, pos, b1, w2, b2, rc, rs1, rs2)


def _group_query_weights(qt, g):
    zero = jnp.zeros((HEAD_DIM, qt.shape[1]), qt.dtype)
    out = []
    for r in range(NSA_GROUP):
        q = qt[r * HEAD_DIM:(r + 1) * HEAD_DIM, :]
        out.append(jnp.concatenate([jnp.where(g == 0, q, zero), jnp.where(g == 1, q, zero)], axis=0))
    return out


def _cmpsel_kernel(qt_ref, kc_ref, vct_ref, mapt_ref, oc_ref, mt_ref, *, tq, n_sel, top):
    g = pl.program_id(1)
    i = pl.program_id(2)
    nr = kc_ref.shape[2]
    kc = kc_ref[0, 0]
    vct = vct_ref[0, 0]
    cend = lax.broadcasted_iota(jnp.int32, (nr, tq), 0) * CMP_STRIDE + (CMP_BLOCK - 1)
    qpos = i * tq + lax.broadcasted_iota(jnp.int32, (nr, tq), 1)
    valid = cend <= qpos
    sts = [_dot(kc, w) for w in _group_query_weights(qt_ref[0], g)]
    vct_ones = _with_ones_rows(vct)
    imp = jnp.zeros((LANES, tq), F32)
    outs = []
    for st in sts:
        st = jnp.where(valid, st, NEG_BIG)
        e = jnp.where(valid, jnp.exp2(st - jnp.max(st, axis=0, keepdims=True)), 0.0).astype(BF16)
        oe = _dot(vct_ones, e)
        den = oe[HEAD_DIM:HEAD_DIM + 1, :]
        inv = jnp.where(den > 0.0, 1.0 / jnp.where(den > 0.0, den, 1.0), 0.0)
        outs.append(oe[0:HEAD_DIM, :] * inv)
        imp = imp + _dot(mapt_ref[...], e) * inv
    oc_ref[0] = jnp.concatenate(outs, axis=0).T

    sid = lax.broadcasted_iota(jnp.int32, (LANES, tq), 0)
    qp = i * tq + lax.broadcasted_iota(jnp.int32, (LANES, tq), 1)
    score0 = jnp.where((sid * SEL_BLOCK > qp) | (sid >= n_sel), -3.0,
                       jnp.where(sid == (qp >> 6), FORCE_SCORE, jnp.where(sid == 0, 2.0 * FORCE_SCORE, imp)))

    score = score0
    for _ in range(top):
        score = jnp.where(score == jnp.max(score, axis=0, keepdims=True), -2.0, score)
    picked = score == -2.0
    mt_ref[0, 0] = jnp.where(picked, 0.0, NEG_BIG).astype(BF16)
    n_picked = jnp.sum(jnp.where(picked, 1.0, 0.0), axis=0, keepdims=True)

    @pl.when(jnp.max(n_picked) > top)
    def _():
        sidf = sid.astype(F32)
        score = score0
        maskt = jnp.full((LANES, tq), NEG_BIG, F32)
        for _ in range(top):
            mx = jnp.max(score, axis=0, keepdims=True)
            idx = jnp.min(jnp.where(score == mx, sidf, float(LANES)), axis=0, keepdims=True)
            pick = (sidf == idx) & (mx > -3.0)
            maskt = jnp.where(pick, 0.0, maskt)
            score = jnp.where(pick, -2.0, score)
        mt_ref[0, 0] = maskt.astype(BF16)


def _cmpsel(qbt, kc, vct, selmap_t):
    B, _, T = qbt.shape
    nr = kc.shape[2]
    tq = NSA_TQ
    n_sel = T // SEL_BLOCK
    return pl.pallas_call(
        functools.partial(_cmpsel_kernel, tq=tq, n_sel=n_sel, top=min(SEL_TOPK, n_sel)),
        grid=(B, NSA_KV_HEADS, T // tq),
        in_specs=[pl.BlockSpec((1, GROUP_W, tq), lambda b, g, i: (b, g, i)),
                  pl.BlockSpec((1, 1, nr, KV_W), lambda b, g, i: (0, b, 0, 0)),
                  pl.BlockSpec((1, 1, HEAD_DIM, nr), lambda b, g, i: (1, b, g, 0)),
                  pl.BlockSpec((LANES, nr), lambda b, g, i: (0, 0))],
        out_specs=[pl.BlockSpec((1, tq, GROUP_W), lambda b, g, i: (b, i, g)),
                   pl.BlockSpec((1, 1, LANES, tq), lambda b, g, i: (b, g, 0, i))],
        out_shape=[jax.ShapeDtypeStruct((B, T, NSA_W), F32),
                   jax.ShapeDtypeStruct((B, NSA_KV_HEADS, LANES, T), BF16)],
        compiler_params=_cparams(("arbitrary", "arbitrary", "arbitrary")),
        name="cmpsel",
    )(qbt, kc, vct, selmap_t)


def _sel_kernel(qt_ref, mt_ref, k_ref, vt_ref, kw_ref, vwt_ref, os_ref, ow_ref, w_scr, *scr, tq, tk):
    g = pl.program_id(1)
    for r, w in enumerate(_group_query_weights(qt_ref[0], g)):
        w_scr[r] = jnp.concatenate([w, mt_ref[0, 0]], axis=0)

    def score_fn(r, start):
        return _dot(k_ref[0, pl.ds(start, tk), :], w_scr[r])

    def vt_fn(r, start):
        return vt_ref[0, :, pl.ds(start, tk)]

    def w_score_fn(r, start):
        return _dot(kw_ref[0, pl.ds(start, tk), :], w_scr[r, 0:KV_W, :])

    def w_vt_fn(r, start):
        return vwt_ref[0, :, pl.ds(start, tk)]

    ost, owt = _causal_flash_t(pl.program_id(2), NSA_GROUP, tq, tk, score_fn, vt_fn, scr,
                               window=(w_score_fn, w_vt_fn))
    os_ref[0] = ost.T
    ow_ref[0] = owt.T


def _sel(qbt, maskt, ks, vst, kw, vwt):
    B, _, T = qbt.shape
    tq, tk = min(SEL_TQ, T), min(SEL_TK, T)
    kaug = KV_W + LANES
    out = pl.BlockSpec((1, tq, GROUP_W), lambda b, g, i: (b, i, g))
    return pl.pallas_call(
        functools.partial(_sel_kernel, tq=tq, tk=tk),
        grid=(B, NSA_KV_HEADS, T // tq),
        in_specs=[pl.BlockSpec((1, GROUP_W, tq), lambda b, g, i: (b, g, i)),
                  pl.BlockSpec((1, 1, LANES, tq), lambda b, g, i: (b, g, 0, i)),
                  pl.BlockSpec((1, T, kaug), lambda b, g, i: (b, 0, 0)),
                  pl.BlockSpec((1, HEAD_DIM, T), lambda b, g, i: (b, g, 0)),
                  pl.BlockSpec((1, T, KV_W), lambda b, g, i: (b, 0, 0)),
                  pl.BlockSpec((1, HEAD_DIM, T), lambda b, g, i: (b, g, 0))],
        out_specs=[out, out],
        out_shape=[jax.ShapeDtypeStruct((B, T, NSA_W), F32)] * 2,
        scratch_shapes=[pltpu.VMEM((NSA_GROUP, kaug, tq), BF16)] + _flash_scratch(NSA_GROUP, tq, tk),
        compiler_params=_cparams(("arbitrary", "arbitrary", "arbitrary")),
        name="sel",
    )(qbt, maskt, ks, vst, kw, vwt)


def _merge_kernel(x_ref, g_ref, oa_ref, oc_ref, os_ref, ow_ref, gb_ref, wg_ref, wof_ref, won_ref, wout_ref, e_ref,
                  o_ref):
    x = x_ref[...]
    h = _rmsnorm(x, g_ref[...]).astype(BF16)
    ya = _dot(oa_ref[...], wof_ref[...])
    sg = _pack3(_sigmoid(gb_ref[...]), 3 * NSA_HEADS)
    on = (_dot(sg, e_ref[0]) * oc_ref[...] + _dot(sg, e_ref[1]) * os_ref[...]
          + _dot(sg, e_ref[2]) * ow_ref[...])
    yb = _dot(on.astype(BF16), won_ref[...])
    mixed = (_sigmoid(_dot(h, wg_ref[:, 0:D_MODEL])) * ya
             + _sigmoid(_dot(h, wg_ref[:, D_MODEL:2 * D_MODEL])) * yb)
    o_ref[...] = x + _dot(mixed.astype(BF16), wout_ref[...])


def _layer_spec(layer, shape):
    return pl.BlockSpec((None,) + shape, lambda *_: (layer,) + (0,) * len(shape))


def _merge(layer, x, g, oa, oc, os_, ow, gb, wp, wof, won, wout, e):
    n = x.shape[0]
    tm = MERGE_TM
    row = lambda w: pl.BlockSpec((tm, w), lambda t: (t, 0))
    const = lambda shape: pl.BlockSpec(shape, lambda t: (0,) * len(shape))
    assert OFF_GATE % (2 * D_MODEL) == 0
    gate_w = pl.BlockSpec((None, D_MODEL, 2 * D_MODEL), lambda t: (layer, 0, OFF_GATE // (2 * D_MODEL)))
    return pl.pallas_call(
        _merge_kernel,
        grid=(n // tm,),
        in_specs=[row(D_MODEL), _layer_spec(layer, (1, D_MODEL)), row(FOX_W), row(NSA_W), row(NSA_W), row(NSA_W),
                  row(LANES), gate_w,
                  _layer_spec(layer, (FOX_W, D_MODEL)), _layer_spec(layer, (NSA_W, D_MODEL)),
                  _layer_spec(layer, (D_MODEL, D_MODEL)), const((3, LANES, NSA_W))],
        out_specs=row(D_MODEL),
        out_shape=jax.ShapeDtypeStruct((n, D_MODEL), F32),
        compiler_params=_cparams(("arbitrary",)),
        name="merge",
    )(x, g, oa, oc, os_, ow, gb, wp, wof, won, wout, e)


def _mlp_kernel(x_ref, g_ref, wu_ref, wd_ref, gf_ref, o_ref, h_scr, *, final_norm):
    x = x_ref[...]
    h_scr[...] = _rmsnorm(x, g_ref[...]).astype(BF16)
    acc = None
    for c0 in range(0, D_FF, MLP_FF_CHUNK):
        u = jnp.maximum(_dot(h_scr[...], wu_ref[:, c0:c0 + MLP_FF_CHUNK]), 0.0)
        part = _dot((u * u).astype(BF16), wd_ref[c0:c0 + MLP_FF_CHUNK, :])
        acc = part if acc is None else acc + part
    y = x + acc
    o_ref[...] = _rmsnorm(y, gf_ref[...]) if final_norm else y


def _mlp(layer, x, g, wu, wd, gf, final_norm):
    n = x.shape[0]
    tm = MLP_TM
    row = pl.BlockSpec((tm, D_MODEL), lambda t: (t, 0))
    const = lambda shape: pl.BlockSpec(shape, lambda t: (0,) * len(shape))
    return pl.pallas_call(
        functools.partial(_mlp_kernel, final_norm=final_norm),
        grid=(n // tm,),
        in_specs=[row, _layer_spec(layer, (1, D_MODEL)), _layer_spec(layer, (D_MODEL, D_FF)),
                  _layer_spec(layer, (D_FF, D_MODEL)), const((1, D_MODEL))],
        out_specs=row,
        out_shape=jax.ShapeDtypeStruct((n, D_MODEL), F32),
        scratch_shapes=[pltpu.VMEM((tm, D_MODEL), BF16)],
        compiler_params=_cparams(("arbitrary",)),
        name="mlp",
    )(x, g, wu, wd, gf)


def _rope_tables(pos):
    inv = ROPE_THETA ** (-jnp.arange(ROPE_HALF, dtype=F32) / ROPE_HALF)
    ang = pos.astype(F32)[:, None] * inv[None, :]
    cos, sin = jnp.cos(ang), jnp.sin(ang)
    n = pos.shape[0]
    rest = HEAD_DIM - ROPE_DIM
    rc = jnp.concatenate([cos, cos, jnp.ones((n, rest), F32)], axis=1)
    rs1 = jnp.concatenate([-sin, jnp.zeros((n, HEAD_DIM - ROPE_HALF), F32)], axis=1)
    rs2 = jnp.concatenate([jnp.zeros((n, ROPE_HALF), F32), sin, jnp.zeros((n, rest), F32)], axis=1)
    two = lambda t: jnp.concatenate([t, t], axis=1)
    return two(rc), two(rs1), two(rs2)


def _pad_heads(w, n_heads):
    zero = jnp.zeros(w.shape[:-1] + (LANES - HEAD_DIM,), w.dtype)
    parts = []
    for h in range(n_heads):
        parts += [w[..., h * HEAD_DIM:(h + 1) * HEAD_DIM], zero]
    return jnp.concatenate(parts, axis=-1)


def _pad_cols(w, width):
    return jnp.concatenate([w, jnp.zeros(w.shape[:-1] + (width - w.shape[-1],), w.dtype)], axis=-1)


def _proj_weight(w_in):
    o = np.cumsum([0, FOX_W, FOX_W, FOX_W, FOX_HEADS, NSA_W, 6 * KV_W, 3 * NSA_HEADS, D_MODEL, D_MODEL])
    cols = lambda j: w_in[..., o[j]:o[j + 1]]
    qscale = SCALE * LOG2E
    parts = [
        _pad_heads(cols(0) * qscale, FOX_HEADS), _pad_heads(cols(1), FOX_HEADS), cols(2),
        _pad_cols(cols(3), LANES),
        cols(4) * qscale,
        cols(5),
        _pad_cols(cols(6), LANES),
        cols(7), cols(8),
    ]
    return jnp.concatenate(parts, axis=-1).astype(BF16)


def _aug_placement():
    pq = np.zeros((LANES, FOX_PAD_W), np.float32)
    pk = np.zeros((LANES, FOX_PAD_W), np.float32)
    for h in range(FOX_HEADS):
        for piece in range(3):
            pq[piece * PIECE_LANES + h, h * LANES + AUG_Q_C + piece] = 1.0
            pk[piece * PIECE_LANES + h, h * LANES + AUG_K_C + piece] = -1.0
    return jnp.asarray(pq, BF16), jnp.asarray(pk, BF16)


def _gate_placement():
    e = np.zeros((3, LANES, NSA_W), np.float32)
    for j in range(3):
        for h in range(NSA_HEADS):
            for piece in range(3):
                e[j, piece * PIECE_LANES + 3 * h + j, h * HEAD_DIM:(h + 1) * HEAD_DIM] = 1.0
    return jnp.asarray(e, BF16)


def _sel_map_t(T, nr):
    nc = (T - CMP_BLOCK) // CMP_STRIDE + 1
    n_sel = T // SEL_BLOCK
    cs = np.arange(nc) * CMP_STRIDE
    ce = cs + CMP_BLOCK
    ss = np.arange(n_sel) * SEL_BLOCK
    se = ss + SEL_BLOCK
    ov = np.clip(np.minimum(ce[None, :], se[:, None]) - np.maximum(cs[None, :], ss[:, None]), 0, None)
    m = np.zeros((LANES, nr), np.float32)
    m[:n_sel, :nc] = ov / CMP_BLOCK
    return jnp.asarray(m, BF16)


def _cmp_weights(cmp_pos, cmp_w1, cmp_b1, cmp_w2, cmp_b2):
    d, half, g2 = cmp_w1.shape[0], CMP_BLOCK // 2, NSA_KV_HEADS

    def group_diag(w, axis):
        z = jnp.zeros_like(w)
        return jnp.stack([jnp.concatenate([w, z], axis=-1), jnp.concatenate([z, w], axis=-1)], axis=axis)

    w1 = group_diag(cmp_w1.astype(BF16).reshape(d, 2, 2, half, HEAD_DIM, CMP_HIDDEN), 4)
    w1 = w1.reshape(d, 2, 2, half * KV_W, g2 * CMP_HIDDEN)
    pos = jnp.concatenate([cmp_pos] * g2, axis=-1)
    b1 = jnp.concatenate([cmp_b1] * g2, axis=-1)[:, :, None, :]
    w2 = group_diag(cmp_w2.astype(BF16), 2).reshape(d, 2, g2 * CMP_HIDDEN, KV_W)
    b2 = jnp.concatenate([cmp_b2] * g2, axis=-1)[:, :, None, :]
    return w1, pos, b1, w2, b2


def kernel(x, norm_mix, w_in, b_forget, cmp_pos, cmp_w1, cmp_b1, cmp_w2, cmp_b2, w_o_fox, w_o_nsa, w_out,
           norm_mlp, w_up, w_down, norm_final):
    B, T, _ = x.shape
    depth = w_in.shape[0]
    nr = T // CMP_STRIDE
    n = B * T
    flat = lambda a: a.reshape(n, a.shape[-1])

    pq, pk = _aug_placement()
    gate_e = _gate_placement()
    selmap_t = _sel_map_t(T, nr)
    rope = _rope_tables(jnp.arange(T))
    crope = [jnp.stack([t, jnp.full_like(t, fill)]) for t, fill in
             zip(_rope_tables(jnp.arange(nr) * CMP_STRIDE + CMP_BLOCK - 1), (1.0, 0.0, 0.0))]
    wp = _proj_weight(w_in)
    bf = _pad_cols(b_forget, LANES)[:, None, :]
    cmp_w = _cmp_weights(cmp_pos, cmp_w1, cmp_b1, cmp_w2, cmp_b2)
    wof, won, wout = w_o_fox.astype(BF16), w_o_nsa.astype(BF16), w_out.astype(BF16)
    wu, wd = w_up.astype(BF16), w_down.astype(BF16)
    g_mix, g_mlp = norm_mix[:, None, :], norm_mlp[:, None, :]

    for l in range(depth):
        qat, ka, vat, qbt, ks, kw, vst, vwt, kvc, gb = _proj(l, x, g_mix, wp, bf, pq, pk, *rope)
        oa = _fox(qat, ka, vat)
        cmp_rows, cmp_cols = _cmp(l, kvc, *cmp_w, *crope)
        oc, maskt = _cmpsel(qbt, cmp_rows, cmp_cols, selmap_t)
        os_, ow = _sel(qbt, maskt, ks, vst, kw, vwt)
        x1 = _merge(l, flat(x), g_mix, flat(oa), flat(oc), flat(os_), flat(ow), flat(gb), wp,
                    wof, won, wout, gate_e)
        x = _mlp(l, x1, g_mlp, wu, wd, norm_final[None, :], l == depth - 1).reshape(B, T, D_MODEL)
    return x
```

```python
import functools

import numpy as np
import jax
import jax.numpy as jnp
from jax import lax
from jax.experimental import pallas as pl
from jax.experimental.pallas import tpu as pltpu

F32 = jnp.float32
BF16 = jnp.bfloat16

D_MODEL = 1024
HEAD_DIM = 64
FOX_HEADS = 8
NSA_HEADS = 8
NSA_KV_HEADS = 2
NSA_GROUP = NSA_HEADS // NSA_KV_HEADS
CMP_BLOCK = 32
CMP_STRIDE = 16
CMP_HIDDEN = 4 * HEAD_DIM
SEL_BLOCK = 64
SEL_TOPK = 16
WINDOW = 512
ROPE_THETA = 500000.0
ROPE_DIM = HEAD_DIM // 4
ROPE_HALF = ROPE_DIM // 2
D_FF = 4 * D_MODEL
RMS_EPS = 1e-6
FORCE_SCORE = 1e6
NEG_BIG = -1e30
SCALE = HEAD_DIM ** -0.5
LOG2E = 1.4426950408889634

FOX_W = FOX_HEADS * HEAD_DIM
NSA_W = NSA_HEADS * HEAD_DIM
KV_W = NSA_KV_HEADS * HEAD_DIM

LANES = 128
DEN_ROWS = 16
PIECE_LANES = 32
VMEM_LIMIT_BYTES = 56 * 1024 * 1024

FOX_PAD_W = FOX_HEADS * LANES
GROUP_W = NSA_GROUP * HEAD_DIM
OFF_QA = 0
OFF_KA = OFF_QA + FOX_PAD_W
OFF_VA = OFF_KA + FOX_PAD_W
OFF_F = OFF_VA + FOX_W
OFF_QB = OFF_F + LANES
OFF_KC = OFF_QB + NSA_W
OFF_VC = OFF_KC + KV_W
OFF_KS = OFF_VC + KV_W
OFF_VS = OFF_KS + KV_W
OFF_KW = OFF_VS + KV_W
OFF_VW = OFF_KW + KV_W
OFF_GB = OFF_VW + KV_W
OFF_GATE = OFF_GB + LANES
PROJ_W = OFF_GATE + 2 * D_MODEL

AUG_Q_C = HEAD_DIM
AUG_K_C = HEAD_DIM + 3

PROJ_TM = 512
FOX_TQ = 512
FOX_TK = 512
FOX_HEADS_PER_STEP = 4
NSA_TQ = 512
SEL_TQ = 512
SEL_TK = 512
MERGE_TM = 512
MLP_TM = 512
MIX_MLP_TM = 512
MLP_FF_CHUNK = 512


def _cparams(sem):
    return pltpu.CompilerParams(dimension_semantics=sem, vmem_limit_bytes=VMEM_LIMIT_BYTES)


def _dot(a, b):
    return jnp.dot(a, b, preferred_element_type=F32)


def _split3(x):
    hi = x.astype(BF16)
    r = x - hi.astype(F32)
    mid = r.astype(BF16)
    lo = (r - mid.astype(F32)).astype(BF16)
    return hi, mid, lo


def _pack3(x, n_real):
    lane = lax.broadcasted_iota(jnp.int32, x.shape, 1)
    hi, mid, lo = _split3(jnp.where(lane < n_real, x, 0.0))
    return (hi.astype(F32) + pltpu.roll(mid.astype(F32), PIECE_LANES, 1)
            + pltpu.roll(lo.astype(F32), 2 * PIECE_LANES, 1)).astype(BF16)


def _lane_tile(x, reps):
    return x if reps == 1 else jnp.concatenate([x] * reps, axis=1)


def _rope(y, rc, rs1, rs2):
    n = y.shape[1]
    return y * rc + pltpu.roll(y, n - ROPE_HALF, 1) * rs1 + pltpu.roll(y, ROPE_HALF, 1) * rs2


def _rmsnorm(x, g):
    return x * lax.rsqrt(jnp.mean(x * x, axis=-1, keepdims=True) + RMS_EPS) * g


def _sigmoid(x):
    return 1.0 / (1.0 + jnp.exp(-x))


def _proj_kernel(x_ref, g_ref, w_ref, bf_ref, pq_ref, pk_ref, rc_ref, rs1_ref, rs2_ref,
                 qa_ref, ka_ref, va_ref, qb_ref, ks_ref, kw_ref, vs_ref, vw_ref, kvc_ref, gb_ref,
                 h_scr, carry_scr, *, tm):
    t = pl.program_id(1)

    @pl.when(t == 0)
    def _():
        carry_scr[...] = jnp.zeros_like(carry_scr)

    h_scr[...] = _rmsnorm(x_ref[0], g_ref[...]).astype(BF16)

    def seg(c0, n):
        return _dot(h_scr[...], w_ref[:, c0:c0 + n])

    f = seg(OFF_F, LANES) + bf_ref[...]
    lf = jnp.minimum(f, 0.0) - jnp.log1p(jnp.exp(-jnp.abs(f)))
    row = lax.broadcasted_iota(jnp.int32, (tm, tm), 0)
    col = lax.broadcasted_iota(jnp.int32, (tm, tm), 1)
    tri = jnp.where(col <= row, 1.0, 0.0).astype(BF16)
    hi, mid, lo = _split3(lf)
    c = _dot(tri, hi) + _dot(tri, mid) + _dot(tri, lo) + carry_scr[0:1, :]
    carry_scr[0:1, :] = c[tm - 1:tm, :]
    c3 = _pack3(c * LOG2E, FOX_HEADS)

    lane = lax.broadcasted_iota(jnp.int32, (tm, FOX_PAD_W), 1) & (LANES - 1)
    ones_q = jnp.where((lane >= AUG_K_C) & (lane < AUG_K_C + 3), 1.0, 0.0)
    ones_k = jnp.where((lane >= AUG_Q_C) & (lane < AUG_Q_C + 3), 1.0, 0.0)
    qa_ref[0] = (seg(OFF_QA, FOX_PAD_W) + _dot(c3, pq_ref[...]) + ones_q).T.astype(BF16)
    ka_ref[0] = (seg(OFF_KA, FOX_PAD_W) + _dot(c3, pk_ref[...]) + ones_k).astype(BF16)
    va_ref[0] = seg(OFF_VA, FOX_W).T.astype(BF16)

    rc, rs1, rs2 = rc_ref[...], rs1_ref[...], rs2_ref[...]
    reps = NSA_W // LANES
    qb_ref[0] = _rope(seg(OFF_QB, NSA_W), _lane_tile(rc, reps), _lane_tile(rs1, reps),
                      _lane_tile(rs2, reps)).T.astype(BF16)
    pos = t * tm + lax.broadcasted_iota(jnp.int32, (tm, LANES), 0)
    onehot = jnp.where((pos >> 6) == lax.broadcasted_iota(jnp.int32, (tm, LANES), 1), 1.0, 0.0)
    ks_ref[0] = jnp.concatenate([_rope(seg(OFF_KS, KV_W), rc, rs1, rs2), onehot], axis=1).astype(BF16)
    kw_ref[0] = _rope(seg(OFF_KW, KV_W), rc, rs1, rs2).astype(BF16)
    vs_ref[0] = seg(OFF_VS, KV_W).T.astype(BF16)
    vw_ref[0] = seg(OFF_VW, KV_W).T.astype(BF16)
    kvc_ref[0, 0] = seg(OFF_KC, KV_W)
    kvc_ref[1, 0] = seg(OFF_VC, KV_W)
    gb_ref[0] = seg(OFF_GB, LANES)


def _proj(layer, x, g, wp, bf, pq, pk, rc, rs1, rs2):
    B, T, _ = x.shape
    tm = PROJ_TM
    row = lambda w: pl.BlockSpec((1, tm, w), lambda b, t: (b, t, 0))
    col = lambda w: pl.BlockSpec((1, w, tm), lambda b, t: (b, 0, t))
    const = lambda shape: pl.BlockSpec(shape, lambda b, t: (0,) * len(shape))
    tab = pl.BlockSpec((tm, LANES), lambda b, t: (t, 0))
    out_shapes = [
        jax.ShapeDtypeStruct((B, FOX_PAD_W, T), BF16),
        jax.ShapeDtypeStruct((B, T, FOX_PAD_W), BF16),
        jax.ShapeDtypeStruct((B, FOX_W, T), BF16),
        jax.ShapeDtypeStruct((B, NSA_W, T), BF16),
        jax.ShapeDtypeStruct((B, T, KV_W + LANES), BF16),
        jax.ShapeDtypeStruct((B, T, KV_W), BF16),
        jax.ShapeDtypeStruct((B, KV_W, T), BF16),
        jax.ShapeDtypeStruct((B, KV_W, T), BF16),
        jax.ShapeDtypeStruct((2, B, T, KV_W), F32),
        jax.ShapeDtypeStruct((B, T, LANES), F32),
    ]
    out_specs = [
        col(FOX_PAD_W), row(FOX_PAD_W), col(FOX_W), col(NSA_W),
        row(KV_W + LANES), row(KV_W), col(KV_W), col(KV_W),
        pl.BlockSpec((2, 1, tm, KV_W), lambda b, t: (0, b, t, 0)),
        row(LANES),
    ]
    return pl.pallas_call(
        functools.partial(_proj_kernel, tm=tm),
        grid=(B, T // tm),
        in_specs=[row(D_MODEL), _layer_spec(layer, (1, D_MODEL)), _layer_spec(layer, (D_MODEL, OFF_GATE)),
                  _layer_spec(layer, (1, LANES)),
                  const((LANES, FOX_PAD_W)), const((LANES, FOX_PAD_W)), tab, tab, tab],
        out_specs=out_specs,
        out_shape=out_shapes,
        scratch_shapes=[pltpu.VMEM((tm, D_MODEL), BF16), pltpu.VMEM((8, LANES), F32)],
        compiler_params=_cparams(("arbitrary", "arbitrary")),
        name="proj",
    )(x, g, wp, bf, pq, pk, rc, rs1, rs2)


def _with_ones_rows(vt):
    first = lax.broadcasted_iota(jnp.int32, (DEN_ROWS, vt.shape[1]), 0) == 0
    return jnp.concatenate([vt, jnp.where(first, 1.0, 0.0).astype(vt.dtype)], axis=0)


def _causal_flash_t(i, n_heads, tq, tk, score_fn, vt_fn, scr, window=None):
    s0_scr, s1_scr, m_scr, acc_scr = scr
    n_full = (i * tq) // tk

    def scores(j, s_scr, fn=score_fn):
        start = pl.multiple_of(j * tk, tk)
        for h in range(n_heads):
            s_scr[h] = fn(h, start)

    def softmax_pv(j, s_scr):
        start = pl.multiple_of(j * tk, tk)
        for h in range(n_heads):
            st = s_scr[h]
            m = m_scr[h]
            m_new = jnp.maximum(m, jnp.max(st, axis=0, keepdims=True))
            p = jnp.exp2(st - m_new).astype(BF16)
            acc_scr[h] = jnp.exp2(m - m_new) * acc_scr[h] + _dot(_with_ones_rows(vt_fn(h, start)), p)
            m_scr[h] = m_new

    def last_tile():
        half = tk // 2
        start = pl.multiple_of(n_full * tk, tk)
        tri = (lax.broadcasted_iota(jnp.int32, (half, tq), 0) <= lax.broadcasted_iota(jnp.int32, (half, tq), 1))
        neg = jnp.full((1, half), NEG_BIG, F32)
        pad = jnp.zeros((HEAD_DIM + DEN_ROWS, half), F32)
        for h in range(n_heads):
            top = jnp.where(tri, s0_scr[h, 0:half, :], NEG_BIG)
            bot = jnp.where(tri[:, 0:half], s0_scr[h, half:tk, half:tq], NEG_BIG)
            m = m_scr[h]
            m_bot = jnp.concatenate([neg, jnp.max(bot, axis=0, keepdims=True)], axis=1)
            m_new = jnp.maximum(jnp.maximum(m, jnp.max(top, axis=0, keepdims=True)), m_bot)
            vt = _with_ones_rows(vt_fn(h, start))
            upd = _dot(vt[:, 0:half], jnp.exp2(top - m_new).astype(BF16))
            upd_right = _dot(vt[:, half:tk], jnp.exp2(bot - m_new[:, half:tq]).astype(BF16))
            acc_scr[h] = (jnp.exp2(m - m_new) * acc_scr[h] + upd + jnp.concatenate([pad, upd_right], axis=1))
            m_scr[h] = m_new

    def normalised(acc_ref):
        return jnp.concatenate([acc_ref[h, 0:HEAD_DIM, :] / acc_ref[h, HEAD_DIM:HEAD_DIM + 1, :]
                                for h in range(n_heads)], axis=0)

    assert tq == tk
    m_scr[...] = jnp.full_like(m_scr, NEG_BIG)
    acc_scr[...] = jnp.zeros_like(acc_scr)

    odd = n_full & 1

    @pl.when(odd == 1)
    def _():
        scores(0, s1_scr)
        scores(1, s0_scr)
        softmax_pv(0, s1_scr)

    @pl.when(odd == 0)
    def _():
        scores(0, s0_scr)

    def pair(t0):
        scores(t0 + 1, s1_scr)
        softmax_pv(t0, s0_scr)
        scores(t0 + 2, s0_scr)
        softmax_pv(t0 + 1, s1_scr)

    n_pairs = n_full // 2
    odd_pair = n_pairs & 1

    @pl.when(odd_pair == 1)
    def _():
        pair(odd)

    def body(jq, carry):
        t0 = odd + 2 * odd_pair + 4 * jq
        pair(t0)
        pair(t0 + 2)
        return carry

    lax.fori_loop(0, n_pairs // 2, body, 0)
    if window is None:
        last_tile()
        return normalised(acc_scr), None

    assert tq == tk == WINDOW
    w_score, w_vt = window
    prev = jnp.maximum(i - 1, 0)
    first_row = jnp.where(i > 0, 0, tk + tq)
    scores(i, s1_scr, w_score)
    last_tile()
    scores(prev, s0_scr, w_score)
    rel = (lax.broadcasted_iota(jnp.int32, (tk, tq), 0) - lax.broadcasted_iota(jnp.int32, (tk, tq), 1))
    cur, start, start_prev = rel <= 0, pl.multiple_of(i * tk, tk), pl.multiple_of(prev * tk, tk)
    outs = []
    for h in range(n_heads):
        st = jnp.where(cur, s1_scr[h], jnp.where(rel > first_row, s0_scr[h], NEG_BIG))
        e = jnp.exp2(st - jnp.max(st, axis=0, keepdims=True))
        oe = (_dot(_with_ones_rows(w_vt(h, start)), jnp.where(cur, e, 0.0).astype(BF16))
              + _dot(_with_ones_rows(w_vt(h, start_prev)), jnp.where(cur, 0.0, e).astype(BF16)))
        outs.append(oe[0:HEAD_DIM, :] / oe[HEAD_DIM:HEAD_DIM + 1, :])
    return normalised(acc_scr), jnp.concatenate(outs, axis=0)


def _flash_scratch(n_heads, tq, tk):
    return [pltpu.VMEM((n_heads, tk, tq), F32), pltpu.VMEM((n_heads, tk, tq), F32),
            pltpu.VMEM((n_heads, 1, tq), F32), pltpu.VMEM((n_heads, HEAD_DIM + DEN_ROWS, tq), F32)]


def _fox_kernel(qt_ref, k_ref, vt_ref, o_ref, *scr, tq, tk, heads):
    def score_fn(h, start):
        return _dot(k_ref[0, pl.ds(start, tk), h * LANES:(h + 1) * LANES], qt_ref[0, h * LANES:(h + 1) * LANES, :])

    def vt_fn(h, start):
        return vt_ref[0, h * HEAD_DIM:(h + 1) * HEAD_DIM, pl.ds(start, tk)]

    ot, _ = _causal_flash_t(pl.program_id(2), heads, tq, tk, score_fn, vt_fn, scr)
    o_ref[0] = ot.T.astype(BF16)


def _fox(qat, ka, vat):
    B, T, _ = ka.shape
    tq, tk = min(FOX_TQ, T), min(FOX_TK, T)
    heads = FOX_HEADS_PER_STEP
    return pl.pallas_call(
        functools.partial(_fox_kernel, tq=tq, tk=tk, heads=heads),
        grid=(B, FOX_HEADS // heads, T // tq),
        in_specs=[pl.BlockSpec((1, heads * LANES, tq), lambda b, h, i: (b, h, i)),
                  pl.BlockSpec((1, T, heads * LANES), lambda b, h, i: (b, 0, h)),
                  pl.BlockSpec((1, heads * HEAD_DIM, T), lambda b, h, i: (b, h, 0))],
        out_specs=pl.BlockSpec((1, tq, heads * HEAD_DIM), lambda b, h, i: (b, i, h)),
        out_shape=jax.ShapeDtypeStruct((B, T, FOX_W), BF16),
        scratch_shapes=_flash_scratch(heads, tq, tk),
        compiler_params=_cparams(("arbitrary", "arbitrary", "arbitrary")),
        name="fox",
    )(qat, ka, vat)


def _cmp_kernel(r_ref, w1_ref, pos_ref, b1_ref, w2_ref, b2_ref, rc_ref, rs1_ref, rs2_ref, o_ref, ot_ref):
    nr = o_ref.shape[0]
    half = CMP_BLOCK // 2
    a = bm = None
    for l in range(half):
        x = r_ref[pl.ds(l, nr, stride=CMP_STRIDE), :]
        ta = _dot((x + pos_ref[l:l + 1, :]).astype(BF16), w1_ref[0, l * KV_W:(l + 1) * KV_W, :])
        tb = _dot((x + pos_ref[half + l:half + l + 1, :]).astype(BF16), w1_ref[1, l * KV_W:(l + 1) * KV_W, :])
        a = ta if a is None else a + ta
        bm = tb if bm is None else bm + tb
    hdn = a + pltpu.roll(bm, nr - 1, 0) + b1_ref[...]
    act = 0.5 * hdn * (1.0 + jnp.tanh(0.7978845608028654 * (hdn + 0.044715 * (hdn * hdn * hdn))))
    y = _rope(_dot(act.astype(BF16), w2_ref[...]) + b2_ref[...], rc_ref[...], rs1_ref[...], rs2_ref[...])
    o_ref[...] = y.astype(BF16)
    ot_ref[...] = y.T.astype(BF16)


def _cmp(layer, kvc, w1, pos, b1, w2, b2, rc, rs1, rs2):
    _, B, T, _ = kvc.shape
    nr = T // CMP_STRIDE
    rw = CMP_STRIDE * KV_W
    per_kv = lambda shape: pl.BlockSpec((None, None) + shape, lambda s, b: (layer, s) + (0,) * len(shape))
    tab = pl.BlockSpec((None, nr, LANES), lambda s, b: (s, 0, 0))
    return pl.pallas_call(
        _cmp_kernel,
        grid=(2, B),
        in_specs=[pl.BlockSpec((None, None, T, KV_W), lambda s, b: (s, b, 0, 0)),
                  per_kv((2, rw, 2 * CMP_HIDDEN)), per_kv((CMP_BLOCK, KV_W)), per_kv((1, 2 * CMP_HIDDEN)),
                  per_kv((2 * CMP_HIDDEN, KV_W)), per_kv((1, KV_W)), tab, tab, tab],
        out_specs=[pl.BlockSpec((None, None, nr, KV_W), lambda s, b: (s, b, 0, 0)),
                   pl.BlockSpec((None, None, KV_W, nr), lambda s, b: (s, b, 0, 0))],
        out_shape=[jax.ShapeDtypeStruct((2, B, nr, KV_W), BF16), jax.ShapeDtypeStruct((2, B, KV_W, nr), BF16)],
        compiler_params=_cparams(("arbitrary", "arbitrary")),
        name="cmp",
    )(kvc, w1, pos, b1, w2, b2, rc, rs1, rs2)


def _group_query_weights(qt, g):
    zero = jnp.zeros((HEAD_DIM, qt.shape[1]), qt.dtype)
    out = []
    for r in range(NSA_GROUP):
        q = qt[r * HEAD_DIM:(r + 1) * HEAD_DIM, :]
        out.append(jnp.concatenate([jnp.where(g == 0, q, zero), jnp.where(g == 1, q, zero)], axis=0))
    return out


def _cmpsel_kernel(qt_ref, kc_ref, vct_ref, mapt_ref, oc_ref, mt_ref, *, tq, n_sel, top):
    g = pl.program_id(1)
    i = pl.program_id(2)
    nr = kc_ref.shape[2]
    kc = kc_ref[0, 0]
    vct = vct_ref[0, 0]
    cend = lax.broadcasted_iota(jnp.int32, (nr, tq), 0) * CMP_STRIDE + (CMP_BLOCK - 1)
    qpos = i * tq + lax.broadcasted_iota(jnp.int32, (nr, tq), 1)
    valid = cend <= qpos
    sts = [_dot(kc, w) for w in _group_query_weights(qt_ref[0], g)]
    vct_ones = _with_ones_rows(vct)
    imp = jnp.zeros((LANES, tq), F32)
    outs = []
    for st in sts:
        st = jnp.where(valid, st, NEG_BIG)
        e = jnp.where(valid, jnp.exp2(st - jnp.max(st, axis=0, keepdims=True)), 0.0).astype(BF16)
        oe = _dot(vct_ones, e)
        den = oe[HEAD_DIM:HEAD_DIM + 1, :]
        inv = jnp.where(den > 0.0, 1.0 / jnp.where(den > 0.0, den, 1.0), 0.0)
        outs.append(oe[0:HEAD_DIM, :] * inv)
        imp = imp + _dot(mapt_ref[...], e) * inv
    oc_ref[0] = jnp.concatenate(outs, axis=0).T

    sid = lax.broadcasted_iota(jnp.int32, (LANES, tq), 0)
    qp = i * tq + lax.broadcasted_iota(jnp.int32, (LANES, tq), 1)
    score0 = jnp.where((sid * SEL_BLOCK > qp) | (sid >= n_sel), -3.0,
                       jnp.where(sid == (qp >> 6), FORCE_SCORE, jnp.where(sid == 0, 2.0 * FORCE_SCORE, imp)))

    score = score0
    for _ in range(top):
        score = jnp.where(score == jnp.max(score, axis=0, keepdims=True), -2.0, score)
    picked = score == -2.0
    mt_ref[0, 0] = jnp.where(picked, 0.0, NEG_BIG).astype(BF16)
    n_picked = jnp.sum(jnp.where(picked, 1.0, 0.0), axis=0, keepdims=True)

    @pl.when(jnp.max(n_picked) > top)
    def _():
        sidf = sid.astype(F32)
        score = score0
        maskt = jnp.full((LANES, tq), NEG_BIG, F32)
        for _ in range(top):
            mx = jnp.max(score, axis=0, keepdims=True)
            idx = jnp.min(jnp.where(score == mx, sidf, float(LANES)), axis=0, keepdims=True)
            pick = (sidf == idx) & (mx > -3.0)
            maskt = jnp.where(pick, 0.0, maskt)
            score = jnp.where(pick, -2.0, score)
        mt_ref[0, 0] = maskt.astype(BF16)


def _cmpsel(qbt, kc, vct, selmap_t):
    B, _, T = qbt.shape
    nr = kc.shape[2]
    tq = NSA_TQ
    n_sel = T // SEL_BLOCK
    return pl.pallas_call(
        functools.partial(_cmpsel_kernel, tq=tq, n_sel=n_sel, top=min(SEL_TOPK, n_sel)),
        grid=(B, NSA_KV_HEADS, T // tq),
        in_specs=[pl.BlockSpec((1, GROUP_W, tq), lambda b, g, i: (b, g, i)),
                  pl.BlockSpec((1, 1, nr, KV_W), lambda b, g, i: (0, b, 0, 0)),
                  pl.BlockSpec((1, 1, HEAD_DIM, nr), lambda b, g, i: (1, b, g, 0)),
                  pl.BlockSpec((LANES, nr), lambda b, g, i: (0, 0))],
        out_specs=[pl.BlockSpec((1, tq, GROUP_W), lambda b, g, i: (b, i, g)),
                   pl.BlockSpec((1, 1, LANES, tq), lambda b, g, i: (b, g, 0, i))],
        out_shape=[jax.ShapeDtypeStruct((B, T, NSA_W), F32),
                   jax.ShapeDtypeStruct((B, NSA_KV_HEADS, LANES, T), BF16)],
        compiler_params=_cparams(("arbitrary", "arbitrary", "arbitrary")),
        name="cmpsel",
    )(qbt, kc, vct, selmap_t)


def _sel_kernel(qt_ref, mt_ref, k_ref, vt_ref, kw_ref, vwt_ref, os_ref, ow_ref, w_scr, *scr, tq, tk):
    g = pl.program_id(1)
    for r, w in enumerate(_group_query_weights(qt_ref[0], g)):
        w_scr[r] = jnp.concatenate([w, mt_ref[0, 0]], axis=0)

    def score_fn(r, start):
        return _dot(k_ref[0, pl.ds(start, tk), :], w_scr[r])

    def vt_fn(r, start):
        return vt_ref[0, :, pl.ds(start, tk)]

    def w_score_fn(r, start):
        return _dot(kw_ref[0, pl.ds(start, tk), :], w_scr[r, 0:KV_W, :])

    def w_vt_fn(r, start):
        return vwt_ref[0, :, pl.ds(start, tk)]

    ost, owt = _causal_flash_t(pl.program_id(2), NSA_GROUP, tq, tk, score_fn, vt_fn, scr,
                               window=(w_score_fn, w_vt_fn))
    os_ref[0] = ost.T
    ow_ref[0] = owt.T


def _sel(qbt, maskt, ks, vst, kw, vwt):
    B, _, T = qbt.shape
    tq, tk = min(SEL_TQ, T), min(SEL_TK, T)
    kaug = KV_W + LANES
    out = pl.BlockSpec((1, tq, GROUP_W), lambda b, g, i: (b, i, g))
    return pl.pallas_call(
        functools.partial(_sel_kernel, tq=tq, tk=tk),
        grid=(B, NSA_KV_HEADS, T // tq),
        in_specs=[pl.BlockSpec((1, GROUP_W, tq), lambda b, g, i: (b, g, i)),
                  pl.BlockSpec((1, 1, LANES, tq), lambda b, g, i: (b, g, 0, i)),
                  pl.BlockSpec((1, T, kaug), lambda b, g, i: (b, 0, 0)),
                  pl.BlockSpec((1, HEAD_DIM, T), lambda b, g, i: (b, g, 0)),
                  pl.BlockSpec((1, T, KV_W), lambda b, g, i: (b, 0, 0)),
                  pl.BlockSpec((1, HEAD_DIM, T), lambda b, g, i: (b, g, 0))],
        out_specs=[out, out],
        out_shape=[jax.ShapeDtypeStruct((B, T, NSA_W), F32)] * 2,
        scratch_shapes=[pltpu.VMEM((NSA_GROUP, kaug, tq), BF16)] + _flash_scratch(NSA_GROUP, tq, tk),
        compiler_params=_cparams(("arbitrary", "arbitrary", "arbitrary")),
        name="sel",
    )(qbt, maskt, ks, vst, kw, vwt)


def _merge_kernel(x_ref, g_ref, oa_ref, oc_ref, os_ref, ow_ref, gb_ref, wg_ref, wof_ref, won_ref, wout_ref, e_ref,
                  o_ref):
    x = x_ref[...]
    h = _rmsnorm(x, g_ref[...]).astype(BF16)
    ya = _dot(oa_ref[...], wof_ref[...])
    sg = _pack3(_sigmoid(gb_ref[...]), 3 * NSA_HEADS)
    on = (_dot(sg, e_ref[0]) * oc_ref[...] + _dot(sg, e_ref[1]) * os_ref[...]
          + _dot(sg, e_ref[2]) * ow_ref[...])
    yb = _dot(on.astype(BF16), won_ref[...])
    mixed = (_sigmoid(_dot(h, wg_ref[:, 0:D_MODEL])) * ya
             + _sigmoid(_dot(h, wg_ref[:, D_MODEL:2 * D_MODEL])) * yb)
    o_ref[...] = x + _dot(mixed.astype(BF16), wout_ref[...])


def _layer_spec(layer, shape):
    return pl.BlockSpec((None,) + shape, lambda *_: (layer,) + (0,) * len(shape))


def _merge(layer, x, g, oa, oc, os_, ow, gb, wp, wof, won, wout, e):
    n = x.shape[0]
    tm = MERGE_TM
    row = lambda w: pl.BlockSpec((tm, w), lambda t: (t, 0))
    const = lambda shape: pl.BlockSpec(shape, lambda t: (0,) * len(shape))
    assert OFF_GATE % (2 * D_MODEL) == 0
    gate_w = pl.BlockSpec((None, D_MODEL, 2 * D_MODEL), lambda t: (layer, 0, OFF_GATE // (2 * D_MODEL)))
    return pl.pallas_call(
        _merge_kernel,
        grid=(n // tm,),
        in_specs=[row(D_MODEL), _layer_spec(layer, (1, D_MODEL)), row(FOX_W), row(NSA_W), row(NSA_W), row(NSA_W),
                  row(LANES), gate_w,
                  _layer_spec(layer, (FOX_W, D_MODEL)), _layer_spec(layer, (NSA_W, D_MODEL)),
                  _layer_spec(layer, (D_MODEL, D_MODEL)), const((3, LANES, NSA_W))],
        out_specs=row(D_MODEL),
        out_shape=jax.ShapeDtypeStruct((n, D_MODEL), F32),
        compiler_params=_cparams(("arbitrary",)),
        name="merge",
    )(x, g, oa, oc, os_, ow, gb, wp, wof, won, wout, e)


def _mlp_kernel(x_ref, g_ref, wu_ref, wd_ref, gf_ref, o_ref, h_scr, *, final_norm):
    x = x_ref[...]
    h_scr[...] = _rmsnorm(x, g_ref[...]).astype(BF16)
    acc = None
    for c0 in range(0, D_FF, MLP_FF_CHUNK):
        u = jnp.maximum(_dot(h_scr[...], wu_ref[:, c0:c0 + MLP_FF_CHUNK]), 0.0)
        part = _dot((u * u).astype(BF16), wd_ref[c0:c0 + MLP_FF_CHUNK, :])
        acc = part if acc is None else acc + part
    y = x + acc
    o_ref[...] = _rmsnorm(y, gf_ref[...]) if final_norm else y


def _mix_mlp_kernel(x_ref, g_ref, oa_ref, oc_ref, os_ref, ow_ref, gb_ref, wg_ref, wof_ref, won_ref, wout_ref, e_ref,
                    g2_ref, wu_ref, wd_ref, gf_ref, o_ref, x1_scr, h_scr, *, final_norm):
    _merge_kernel(x_ref, g_ref, oa_ref, oc_ref, os_ref, ow_ref, gb_ref, wg_ref, wof_ref, won_ref, wout_ref, e_ref,
                  x1_scr)
    _mlp_kernel(x1_scr, g2_ref, wu_ref, wd_ref, gf_ref, o_ref, h_scr, final_norm=final_norm)


def _mix_mlp(layer, x, g, oa, oc, os_, ow, gb, wp, wof, won, wout, e, g2, wu, wd, gf, final_norm):
    n = x.shape[0]
    tm = MIX_MLP_TM
    row = lambda w: pl.BlockSpec((tm, w), lambda t: (t, 0))
    once = lambda shape: pl.BlockSpec((None,) + shape, lambda t: (layer,) + (0,) * len(shape),
                                      pipeline_mode=pl.Buffered(1))
    assert OFF_GATE % (2 * D_MODEL) == 0
    gate_w = pl.BlockSpec((None, D_MODEL, 2 * D_MODEL), lambda t: (layer, 0, OFF_GATE // (2 * D_MODEL)),
                          pipeline_mode=pl.Buffered(1))
    return pl.pallas_call(
        functools.partial(_mix_mlp_kernel, final_norm=final_norm),
        grid=(n // tm,),
        in_specs=[row(D_MODEL), once((1, D_MODEL)), row(FOX_W), row(NSA_W), row(NSA_W), row(NSA_W),
                  row(LANES), gate_w, once((FOX_W, D_MODEL)), once((NSA_W, D_MODEL)), once((D_MODEL, D_MODEL)),
                  pl.BlockSpec((3, LANES, NSA_W), lambda t: (0, 0, 0), pipeline_mode=pl.Buffered(1)),
                  once((1, D_MODEL)), once((D_MODEL, D_FF)), once((D_FF, D_MODEL)),
                  pl.BlockSpec((1, D_MODEL), lambda t: (0, 0))],
        out_specs=row(D_MODEL),
        out_shape=jax.ShapeDtypeStruct((n, D_MODEL), F32),
        scratch_shapes=[pltpu.VMEM((tm, D_MODEL), F32), pltpu.VMEM((tm, D_MODEL), BF16)],
        compiler_params=_cparams(("arbitrary",)),
        name="mix_mlp",
    )(x, g, oa, oc, os_, ow, gb, wp, wof, won, wout, e, g2, wu, wd, gf)


def _mlp(layer, x, g, wu, wd, gf, final_norm):
    n = x.shape[0]
    tm = MLP_TM
    row = pl.BlockSpec((tm, D_MODEL), lambda t: (t, 0))
    const = lambda shape: pl.BlockSpec(shape, lambda t: (0,) * len(shape))
    return pl.pallas_call(
        functools.partial(_mlp_kernel, final_norm=final_norm),
        grid=(n // tm,),
        in_specs=[row, _layer_spec(layer, (1, D_MODEL)), _layer_spec(layer, (D_MODEL, D_FF)),
                  _layer_spec(layer, (D_FF, D_MODEL)), const((1, D_MODEL))],
        out_specs=row,
        out_shape=jax.ShapeDtypeStruct((n, D_MODEL), F32),
        scratch_shapes=[pltpu.VMEM((tm, D_MODEL), BF16)],
        compiler_params=_cparams(("arbitrary",)),
        name="mlp",
    )(x, g, wu, wd, gf)


def _rope_tables(pos):
    inv = ROPE_THETA ** (-jnp.arange(ROPE_HALF, dtype=F32) / ROPE_HALF)
    ang = pos.astype(F32)[:, None] * inv[None, :]
    cos, sin = jnp.cos(ang), jnp.sin(ang)
    n = pos.shape[0]
    rest = HEAD_DIM - ROPE_DIM
    rc = jnp.concatenate([cos, cos, jnp.ones((n, rest), F32)], axis=1)
    rs1 = jnp.concatenate([-sin, jnp.zeros((n, HEAD_DIM - ROPE_HALF), F32)], axis=1)
    rs2 = jnp.concatenate([jnp.zeros((n, ROPE_HALF), F32), sin, jnp.zeros((n, rest), F32)], axis=1)
    two = lambda t: jnp.concatenate([t, t], axis=1)
    return two(rc), two(rs1), two(rs2)


def _pad_heads(w, n_heads):
    zero = jnp.zeros(w.shape[:-1] + (LANES - HEAD_DIM,), w.dtype)
    parts = []
    for h in range(n_heads):
        parts += [w[..., h * HEAD_DIM:(h + 1) * HEAD_DIM], zero]
    return jnp.concatenate(parts, axis=-1)


def _pad_cols(w, width):
    return jnp.concatenate([w, jnp.zeros(w.shape[:-1] + (width - w.shape[-1],), w.dtype)], axis=-1)


def _proj_weight(w_in):
    o = np.cumsum([0, FOX_W, FOX_W, FOX_W, FOX_HEADS, NSA_W, 6 * KV_W, 3 * NSA_HEADS, D_MODEL, D_MODEL])
    cols = lambda j: w_in[..., o[j]:o[j + 1]]
    qscale = SCALE * LOG2E
    parts = [
        _pad_heads(cols(0) * qscale, FOX_HEADS), _pad_heads(cols(1), FOX_HEADS), cols(2),
        _pad_cols(cols(3), LANES),
        cols(4) * qscale,
        cols(5),
        _pad_cols(cols(6), LANES),
        cols(7), cols(8),
    ]
    return jnp.concatenate(parts, axis=-1).astype(BF16)


def _aug_placement():
    pq = np.zeros((LANES, FOX_PAD_W), np.float32)
    pk = np.zeros((LANES, FOX_PAD_W), np.float32)
    for h in range(FOX_HEADS):
        for piece in range(3):
            pq[piece * PIECE_LANES + h, h * LANES + AUG_Q_C + piece] = 1.0
            pk[piece * PIECE_LANES + h, h * LANES + AUG_K_C + piece] = -1.0
    return jnp.asarray(pq, BF16), jnp.asarray(pk, BF16)


def _gate_placement():
    e = np.zeros((3, LANES, NSA_W), np.float32)
    for j in range(3):
        for h in range(NSA_HEADS):
            for piece in range(3):
                e[j, piece * PIECE_LANES + 3 * h + j, h * HEAD_DIM:(h + 1) * HEAD_DIM] = 1.0
    return jnp.asarray(e, BF16)


def _sel_map_t(T, nr):
    nc = (T - CMP_BLOCK) // CMP_STRIDE + 1
    n_sel = T // SEL_BLOCK
    cs = np.arange(nc) * CMP_STRIDE
    ce = cs + CMP_BLOCK
    ss = np.arange(n_sel) * SEL_BLOCK
    se = ss + SEL_BLOCK
    ov = np.clip(np.minimum(ce[None, :], se[:, None]) - np.maximum(cs[None, :], ss[:, None]), 0, None)
    m = np.zeros((LANES, nr), np.float32)
    m[:n_sel, :nc] = ov / CMP_BLOCK
    return jnp.asarray(m, BF16)


def _cmp_weights(cmp_pos, cmp_w1, cmp_b1, cmp_w2, cmp_b2):
    d, half, g2 = cmp_w1.shape[0], CMP_BLOCK // 2, NSA_KV_HEADS

    def group_diag(w, axis):
        z = jnp.zeros_like(w)
        return jnp.stack([jnp.concatenate([w, z], axis=-1), jnp.concatenate([z, w], axis=-1)], axis=axis)

    w1 = group_diag(cmp_w1.astype(BF16).reshape(d, 2, 2, half, HEAD_DIM, CMP_HIDDEN), 4)
    w1 = w1.reshape(d, 2, 2, half * KV_W, g2 * CMP_HIDDEN)
    pos = jnp.concatenate([cmp_pos] * g2, axis=-1)
    b1 = jnp.concatenate([cmp_b1] * g2, axis=-1)[:, :, None, :]
    w2 = group_diag(cmp_w2.astype(BF16), 2).reshape(d, 2, g2 * CMP_HIDDEN, KV_W)
    b2 = jnp.concatenate([cmp_b2] * g2, axis=-1)[:, :, None, :]
    return w1, pos, b1, w2, b2


def kernel(x, norm_mix, w_in, b_forget, cmp_pos, cmp_w1, cmp_b1, cmp_w2, cmp_b2, w_o_fox, w_o_nsa, w_out,
           norm_mlp, w_up, w_down, norm_final):
    B, T, _ = x.shape
    depth = w_in.shape[0]
    nr = T // CMP_STRIDE
    n = B * T
    flat = lambda a: a.reshape(n, a.shape[-1])

    pq, pk = _aug_placement()
    gate_e = _gate_placement()
    selmap_t = _sel_map_t(T, nr)
    rope = _rope_tables(jnp.arange(T))
    crope = [jnp.stack([t, jnp.full_like(t, fill)]) for t, fill in
             zip(_rope_tables(jnp.arange(nr) * CMP_STRIDE + CMP_BLOCK - 1), (1.0, 0.0, 0.0))]
    wp = _proj_weight(w_in)
    bf = _pad_cols(b_forget, LANES)[:, None, :]
    cmp_w = _cmp_weights(cmp_pos, cmp_w1, cmp_b1, cmp_w2, cmp_b2)
    wof, won, wout = w_o_fox.astype(BF16), w_o_nsa.astype(BF16), w_out.astype(BF16)
    wu, wd = w_up.astype(BF16), w_down.astype(BF16)
    g_mix, g_mlp = norm_mix[:, None, :], norm_mlp[:, None, :]

    for l in range(depth):
        qat, ka, vat, qbt, ks, kw, vst, vwt, kvc, gb = _proj(l, x, g_mix, wp, bf, pq, pk, *rope)
        oa = _fox(qat, ka, vat)
        cmp_rows, cmp_cols = _cmp(l, kvc, *cmp_w, *crope)
        oc, maskt = _cmpsel(qbt, cmp_rows, cmp_cols, selmap_t)
        os_, ow = _sel(qbt, maskt, ks, vst, kw, vwt)
        x = _mix_mlp(l, flat(x), g_mix, flat(oa), flat(oc), flat(os_), flat(ow), flat(gb), wp, wof, won, wout,
                     gate_e, g_mlp, wu, wd, norm_final[None, :], l == depth - 1).reshape(B, T, D_MODEL)
    return x
```
